```python
import math
import jax, jax.numpy as jnp
from jax import lax
import numpy as np

D_MODEL = 1024
BATCH = 16
SEQ = 2048
DEPTH = 4

HEAD_DIM = 64
D_MIX = D_MODEL
DIFF_HEADS = (3 * D_MIX) // (8 * HEAD_DIM)
DIFF_DV = HEAD_DIM
DIFF_DK = HEAD_DIM // 2
DSA_HEADS = (3 * D_MIX) // (8 * HEAD_DIM)
DSA_DH = HEAD_DIM
DSA_LATENT = 2 * HEAD_DIM
N_IDX_HEADS = 8
D_IDX = 64
DSA_TOPK_MAX = 256
RET_HEADS = D_MIX // HEAD_DIM - DIFF_HEADS - DSA_HEADS
RET_DK = HEAD_DIM
RET_DV = HEAD_DIM
RET_CHUNK = 128
Q_BLOCK = 128
ROPE_BASE = 10000.0
NORM_EPS = 1e-6
NEG = -1e30
IDX_SCALE = (N_IDX_HEADS ** -0.5) * (D_IDX ** -0.5)

IN_SIZES = (
    DIFF_HEADS * 2 * DIFF_DK,
    DIFF_HEADS * 2 * DIFF_DK,
    DIFF_HEADS * DIFF_DV,
    DSA_HEADS * DSA_DH,
    DSA_LATENT,
    N_IDX_HEADS * D_IDX,
    D_IDX,
    N_IDX_HEADS,
    RET_HEADS * RET_DK,
    RET_HEADS * RET_DK,
    RET_HEADS * RET_DV,
    D_MIX,
)
IN_COLS = sum(IN_SIZES)

kernel_name = "hymba_style_diff_dsa_retention_trunk"


def rms_norm(x, g):
    xf = x.astype(jnp.float32)
    y = xf * lax.rsqrt(jnp.mean(xf * xf, axis=-1, keepdims=True) + NORM_EPS)
    return (y * g.astype(jnp.float32)).astype(x.dtype)


def to_blocks(a):
    B, S = a.shape[:2]
    return jnp.moveaxis(a.reshape((B, S // Q_BLOCK, Q_BLOCK) + a.shape[2:]), 1, 0)


def from_blocks(a):
    a = jnp.moveaxis(a, 0, 1)
    return a.reshape((a.shape[0], a.shape[1] * a.shape[2]) + a.shape[3:])


def rotary(x, cos, sin):
    x1, x2 = jnp.split(x.astype(jnp.float32), 2, axis=-1)
    c = cos[None, :, None, :]
    s = sin[None, :, None, :]
    return jnp.concatenate([x1 * c - x2 * s, x1 * s + x2 * c], axis=-1).astype(x.dtype)


def diff_attention(q, k, v, lam_vecs, norm_g, lam_init):
    S = q.shape[1]
    lf = lam_vecs.astype(jnp.float32)
    lam = jnp.exp(jnp.sum(lf[0] * lf[1])) - jnp.exp(jnp.sum(lf[2] * lf[3])) + lam_init
    scale = q.shape[-1] ** -0.5
    kpos = jnp.arange(S)

    def block(args):
        qb, start = args
        qpos = start + jnp.arange(Q_BLOCK)
        causal = kpos[None, :] <= qpos[:, None]
        s = jnp.einsum('bqhcd,bkhcd->bhcqk', qb, k).astype(jnp.float32) * scale
        p = jax.nn.softmax(jnp.where(causal, s, NEG), axis=-1)
        a = p[:, :, 0] - lam * p[:, :, 1]
        return jnp.einsum('bhqk,bkhd->bqhd', a.astype(v.dtype), v)

    starts = jnp.arange(S // Q_BLOCK) * Q_BLOCK
    o = from_blocks(lax.map(block, (to_blocks(q), starts)))
    return rms_norm(o, norm_g) * (1.0 - lam_init)


def dsa_attention(q, c_kv, iq, ik, iw, w_uk, w_uv, top_k):
    S = q.shape[1]
    scale = DSA_DH ** -0.5
    q_lat = jnp.einsum('bshd,hdr->bshr', q, w_uk)
    kpos = jnp.arange(S)

    def block(args):
        qlb, iqb, iwb, start = args
        qpos = start + jnp.arange(Q_BLOCK)
        causal = kpos[None, :] <= qpos[:, None]
        rel = jax.nn.relu(jnp.einsum('bqhd,bsd->bqhs', iqb, ik).astype(jnp.float32))
        score = jnp.einsum('bqhs,bqh->bqs', rel, iwb.astype(jnp.float32)) * IDX_SCALE
        score = jnp.where(causal[None], score, NEG)
        _, sel = lax.top_k(score, top_k)
        valid = sel <= qpos[None, :, None]
        c_sel = jax.vmap(lambda c, s: c[s])(c_kv, sel)
        s = jnp.einsum('bqhr,bqkr->bhqk', qlb, c_sel).astype(jnp.float32) * scale
        p = jax.nn.softmax(jnp.where(valid[:, None], s, NEG), axis=-1)
        o_lat = jnp.einsum('bhqk,bqkr->bqhr', p.astype(c_sel.dtype), c_sel)
        return jnp.einsum('bqhr,hrd->bqhd', o_lat, w_uv)

    starts = jnp.arange(S // Q_BLOCK) * Q_BLOCK
    o = lax.map(block, (to_blocks(q_lat), to_blocks(iq), to_blocks(iw), starts))
    return from_blocks(o)


def retention(q, k, v, norm_g):
    B, S, H, DK = q.shape
    DV = v.shape[-1]
    C = RET_CHUNK
    log_g = jnp.log(1.0 - 2.0 ** (-5.0 - jnp.arange(H, dtype=jnp.float32)))
    pos = jnp.arange(C, dtype=jnp.float32)
    diff = pos[:, None] - pos[None, :]
    d_intra = jnp.where(diff >= 0, jnp.exp(jnp.maximum(diff, 0.0)[None] * log_g[:, None, None]), 0.0)
    xi = jnp.exp((pos + 1.0)[:, None] * log_g[None, :])
    zeta = jnp.exp((C - 1.0 - pos)[:, None] * log_g[None, :])
    g_chunk = jnp.exp(C * log_g)

    def chunks(a):
        return jnp.moveaxis(a.astype(jnp.float32).reshape((B, S // C, C) + a.shape[2:]), 1, 0)

    qc_all = chunks(q)
    kc_all = chunks(k) * (DK ** -0.5)
    vc_all = chunks(v)

    def step(R, inp):
        qc, kc, vc = inp
        att = jnp.einsum('bqhd,bkhd->bhqk', qc, kc) * d_intra[None]
        inner = jnp.einsum('bhqk,bkhe->bqhe', att, vc)
        cross = jnp.einsum('bqhd,bhde->bqhe', qc, R) * xi[None, :, :, None]
        R_new = g_chunk[None, :, None, None] * R + jnp.einsum('bkhd,bkhe->bhde', kc * zeta[None, :, :, None], vc)
        return R_new, inner + cross

    R0 = jnp.zeros((B, H, DK, DV), jnp.float32)
    _, o = lax.scan(step, R0, (qc_all, kc_all, vc_all))
    o = jnp.moveaxis(o, 0, 1).reshape(B, S, H, DV).astype(v.dtype)
    return rms_norm(o, norm_g)


def setup_inputs(seed: int = 0) -> dict:
    key = jax.random.key(seed)
    ks = jax.random.split(key, 11)
    f32 = jnp.float32
    x = jax.random.normal(ks[0], (BATCH, SEQ, D_MODEL), f32)
    attn_norm = 1.0 + 0.02 * jax.random.normal(ks[1], (DEPTH, D_MODEL), f32)
    w_in = jax.random.normal(ks[2], (DEPTH, D_MODEL, IN_COLS), f32) * D_MODEL ** -0.5
    diff_lambda = 0.1 * jax.random.normal(ks[3], (DEPTH, 4, DIFF_DK), f32)
    diff_norm = 1.0 + 0.02 * jax.random.normal(ks[4], (DEPTH, DIFF_DV), f32)
    kv_norm = 1.0 + 0.02 * jax.random.normal(ks[5], (DEPTH, DSA_LATENT), f32)
    w_uk = jax.random.normal(ks[6], (DEPTH, DSA_HEADS, DSA_DH, DSA_LATENT), f32) * DSA_LATENT ** -0.5
    w_uv = jax.random.normal(ks[7], (DEPTH, DSA_HEADS, DSA_LATENT, DSA_DH), f32) * DSA_LATENT ** -0.5
    ret_norm = 1.0 + 0.02 * jax.random.normal(ks[8], (DEPTH, RET_DV), f32)
    w_out = jax.random.normal(ks[9], (DEPTH, D_MIX, D_MODEL), f32) * (D_MIX ** -0.5) * (2 * DEPTH) ** -0.5
    final_norm = 1.0 + 0.02 * jax.random.normal(ks[10], (D_MODEL,), f32)
    return {"x": x, "attn_norm": attn_norm, "w_in": w_in, "diff_lambda": diff_lambda,
            "diff_norm": diff_norm, "kv_norm": kv_norm, "w_uk": w_uk, "w_uv": w_uv,
            "ret_norm": ret_norm, "w_out": w_out, "final_norm": final_norm}


def reference(x, attn_norm, w_in, diff_lambda, diff_norm, kv_norm, w_uk, w_uv, ret_norm, w_out, final_norm):
    B, S, _ = x.shape
    top_k = min(DSA_TOPK_MAX, S // 4)
    split_points = np.cumsum(np.array(IN_SIZES))[:-1].tolist()
    inv_freq = ROPE_BASE ** (-jnp.arange(RET_DK // 2, dtype=jnp.float32) / (RET_DK // 2))
    ang = jnp.arange(S, dtype=jnp.float32)[:, None] * inv_freq[None, :]
    cos, sin = jnp.cos(ang), jnp.sin(ang)

    h = x
    for layer in range(DEPTH):
        u = rms_norm(h, attn_norm[layer])
        proj = jnp.einsum('bsd,dc->bsc', u, w_in[layer])
        (dq, dk, dv, sq, ckv, iq, ik, iw, rq, rk, rv, gate) = jnp.split(proj, split_points, axis=-1)

        lam_init = 0.8 - 0.6 * math.exp(-0.3 * layer)
        o_a = diff_attention(dq.reshape(B, S, DIFF_HEADS, 2, DIFF_DK),
                             dk.reshape(B, S, DIFF_HEADS, 2, DIFF_DK),
                             dv.reshape(B, S, DIFF_HEADS, DIFF_DV),
                             diff_lambda[layer], diff_norm[layer], lam_init)

        o_b = dsa_attention(sq.reshape(B, S, DSA_HEADS, DSA_DH),
                            rms_norm(ckv, kv_norm[layer]),
                            iq.reshape(B, S, N_IDX_HEADS, D_IDX), ik, iw,
                            w_uk[layer], w_uv[layer], top_k)

        o_c = retention(rotary(rq.reshape(B, S, RET_HEADS, RET_DK), cos, sin),
                        rotary(rk.reshape(B, S, RET_HEADS, RET_DK), cos, sin),
                        rv.reshape(B, S, RET_HEADS, RET_DV), ret_norm[layer])

        mixed = jnp.concatenate([o_a.reshape(B, S, -1), o_b.reshape(B, S, -1), o_c.reshape(B, S, -1)], axis=-1)
        y = jax.nn.silu(gate) * mixed
        h = h + jnp.einsum('bsc,cd->bsd', y, w_out[layer])

    return rms_norm(h, final_norm)
```

```python
import functools
import math

import numpy as np
import jax
import jax.numpy as jnp
from jax import lax
from jax.experimental import pallas as pl
from jax.experimental.pallas import tpu as pltpu

D_MODEL = 1024
DEPTH = 4
DIFF_HEADS = 6
DIFF_DK = 32
DIFF_DV = 64
DSA_HEADS = 6
DSA_DH = 64
DSA_LATENT = 128
N_IDX_HEADS = 8
D_IDX = 64
DSA_TOPK_MAX = 256
RET_HEADS = 4
RET_DK = 64
RET_DV = 64
RET_CHUNK = 128
ROPE_BASE = 10000.0
NORM_EPS = 1e-6
NEG = -1e30
IDX_SCALE = (N_IDX_HEADS ** -0.5) * (D_IDX ** -0.5)
LOG2E = 1.4426950408889634

W_DIFF = DIFF_HEADS * 2 * DIFF_DK
W_DSA = DSA_HEADS * DSA_DH
W_IDX = N_IDX_HEADS * D_IDX
W_RET = RET_HEADS * RET_DK
W_QLAT = DSA_HEADS * DSA_LATENT

C_DQ = 0
C_DK = C_DQ + W_DIFF
C_DV = C_DK + W_DIFF
C_SQ = C_DV + W_DIFF
C_CKV = C_SQ + W_DSA
C_IQ = C_CKV + DSA_LATENT
C_IK = C_IQ + W_IDX
C_RQ = C_IK + 128
C_RK = C_RQ + W_RET
C_RV = C_RK + W_RET
C_GATE = C_RV + W_RET
C_END = C_GATE + D_MODEL

LANES = 128
TILE = 256
TM = 512
VMEM_LIMIT = 56 * 1024 * 1024
MASKED = -3.0e38
M_INIT = -1.0e30
INT_MIN = -2 ** 31

_NT = (((1,), (1,)), ((), ()))


def _neg_key():
    bits = int(np.array(NEG, np.float32).view(np.int32))
    return bits ^ ((bits >> 31) & 0x7FFFFFFF)


NEG_KEY = _neg_key()


def _proj_kernel(h_ref, gain_ref, w_ref, wiw_ref, wuk_ref, kvg_ref, rc_ref, rs_ref,
                 dq_ref, dk_ref, dvt_ref, qlat_ref, ckv_ref, ckvt_ref, iq_ref, ika_ref, ikb_ref,
                 iwt_ref, rq_ref, rk_ref, rv_ref, g_ref):
    f32, bf16 = jnp.float32, jnp.bfloat16
    h = h_ref[...]
    ms = jnp.mean(h * h, axis=-1, keepdims=True)
    u = (h * lax.rsqrt(ms + NORM_EPS) * gain_ref[...]).astype(bf16)
    n_sub = h.shape[0] // TILE

    def proj(lo, hi):
        return jnp.dot(u, w_ref[:, lo:hi], preferred_element_type=f32)

    dq_ref[...] = (proj(C_DQ, C_DK) * (DIFF_DK ** -0.5 * LOG2E)).astype(bf16)
    dk_ref[...] = proj(C_DK, C_DV).astype(bf16)
    dv = proj(C_DV, C_SQ)
    for t in range(n_sub):
        dvt_ref[t] = dv[t * TILE:(t + 1) * TILE].T.astype(bf16)

    sq = proj(C_SQ, C_CKV).astype(bf16)
    qlat = jnp.dot(sq, wuk_ref[...], preferred_element_type=f32)
    qlat_ref[...] = (qlat * (DSA_DH ** -0.5 * LOG2E)).astype(bf16)
    ckv = proj(C_CKV, C_IQ)
    cms = jnp.mean(ckv * ckv, axis=-1, keepdims=True)
    c = ckv * lax.rsqrt(cms + NORM_EPS) * kvg_ref[...]
    ckv_ref[...] = c.astype(bf16)
    for t in range(n_sub):
        ckvt_ref[t] = c[t * TILE:(t + 1) * TILE].T.astype(bf16)

    iq_ref[...] = proj(C_IQ, C_IK).astype(bf16)
    ik = proj(C_IK, C_RQ)
    ika_ref[...] = ik.astype(bf16)
    ikb_ref[...] = pltpu.roll(ik, D_IDX, 1).astype(bf16)
    iwt = lax.dot_general(wiw_ref[...], u, _NT, preferred_element_type=f32) * IDX_SCALE
    for t in range(n_sub):
        iwt_ref[t] = iwt[:, t * TILE:(t + 1) * TILE]

    lane = lax.broadcasted_iota(jnp.int32, (h.shape[0], LANES), 1)
    first_half = (lane % RET_DK) < (RET_DK // 2)

    def rotary(x):
        parts = []
        for a in range(W_RET // LANES):
            xa = x[:, a * LANES:(a + 1) * LANES]
            partner = jnp.where(first_half, pltpu.roll(xa, LANES - RET_DK // 2, 1),
                                pltpu.roll(xa, RET_DK // 2, 1))
            sl = slice(a * LANES, (a + 1) * LANES)
            parts.append(xa * rc_ref[:, sl] + partner * rs_ref[:, sl])
        return jnp.concatenate(parts, axis=1)

    rq_ref[...] = rotary(proj(C_RQ, C_RK)).astype(bf16)
    rk_ref[...] = (rotary(proj(C_RK, C_RV)) * (RET_DK ** -0.5)).astype(bf16)
    rv_ref[...] = proj(C_RV, C_GATE).astype(bf16)

    gate = proj(C_GATE, C_END)
    g_ref[...] = gate / (1.0 + jnp.exp(-gate))


def _proj_call(h, gain, w, wiw, wuk, kvg, rc, rs, seq):
    n = h.shape[0]
    nt = n // TILE
    sub = TM // TILE
    pos_blocks = seq // TM
    bf16, f32 = jnp.bfloat16, jnp.float32
    row = lambda width: pl.BlockSpec((TM, width), lambda i: (i, 0))
    full = lambda a: pl.BlockSpec(a.shape, lambda i: (0,) * a.ndim)
    tposed = lambda rows: pl.BlockSpec((sub, rows, TILE), lambda i: (i, 0, 0))
    out_shape = (
        jax.ShapeDtypeStruct((n, W_DIFF), bf16),
        jax.ShapeDtypeStruct((n, W_DIFF), bf16),
        jax.ShapeDtypeStruct((nt, W_DIFF, TILE), bf16),
        jax.ShapeDtypeStruct((n, W_QLAT), bf16),
        jax.ShapeDtypeStruct((n, DSA_LATENT), bf16),
        jax.ShapeDtypeStruct((nt, DSA_LATENT, TILE), bf16),
        jax.ShapeDtypeStruct((n, W_IDX), bf16),
        jax.ShapeDtypeStruct((n, LANES), bf16),
        jax.ShapeDtypeStruct((n, LANES), bf16),
        jax.ShapeDtypeStruct((nt, 16, TILE), f32),
        jax.ShapeDtypeStruct((n, W_RET), bf16),
        jax.ShapeDtypeStruct((n, W_RET), bf16),
        jax.ShapeDtypeStruct((n, W_RET), bf16),
        jax.ShapeDtypeStruct((n, D_MODEL), f32),
    )
    out_specs = (
        row(W_DIFF), row(W_DIFF), tposed(W_DIFF), row(W_QLAT), row(DSA_LATENT), tposed(DSA_LATENT),
        row(W_IDX), row(LANES), row(LANES), tposed(16), row(W_RET), row(W_RET), row(W_RET), row(D_MODEL),
    )
    tab = pl.BlockSpec((TM, W_RET), lambda i: (i % pos_blocks, 0))
    return pl.pallas_call(
        _proj_kernel,
        grid=(n // TM,),
        in_specs=[row(D_MODEL), full(gain), full(w), full(wiw), full(wuk), full(kvg), tab, tab],
        out_specs=out_specs,
        out_shape=out_shape,
        compiler_params=pltpu.CompilerParams(dimension_semantics=("parallel",),
                                             vmem_limit_bytes=VMEM_LIMIT),
        name="proj",
    )(h, gain, w, wiw, wuk, kvg, rc, rs)


def _diff_kernel(lam_ref, q_ref, k_ref, vt_ref, ng_ref, o_ref, acc_ref, m_ref, l_ref, *, lam_init):
    f32, bf16 = jnp.float32, jnp.bfloat16
    qi = pl.program_id(2)
    q = q_ref[0]
    lane = lax.broadcasted_iota(jnp.int32, q.shape, 1)
    qm = [jnp.where((lane >= DIFF_DK * c) & (lane < DIFF_DK * (c + 1)), q, jnp.zeros_like(q))
          for c in range(4)]
    rel = (lax.broadcasted_iota(jnp.int32, (TILE, TILE), 0)
           - lax.broadcasted_iota(jnp.int32, (TILE, TILE), 1))
    m_ref[...] = jnp.full(m_ref.shape, M_INIT, f32)
    l_ref[...] = jnp.zeros(l_ref.shape, f32)
    acc_ref[...] = jnp.zeros(acc_ref.shape, f32)

    def body(j, carry):
        k_j = k_ref[0, j]
        vt_j = vt_ref[0, j]
        causal = rel <= (qi - j) * TILE
        for c in range(4):
            s = lax.dot_general(k_j, qm[c], _NT, preferred_element_type=f32)
            s = jnp.where(causal, s, MASKED)
            m_old = m_ref[c]
            m_new = jnp.maximum(m_old, jnp.max(s, axis=0, keepdims=True))
            alpha = jnp.exp2(m_old - m_new)
            p = jnp.exp2(s - m_new)
            l_ref[c] = alpha * l_ref[c] + jnp.sum(p, axis=0, keepdims=True)
            v_h = vt_j[(c // 2) * DIFF_DV:(c // 2 + 1) * DIFF_DV, :]
            acc_ref[c] = alpha * acc_ref[c] + jnp.dot(v_h, p.astype(bf16), preferred_element_type=f32)
            m_ref[c] = m_new
        return carry

    lax.fori_loop(0, qi + 1, body, 0)

    lv = lam_ref[...]
    lam = (jnp.exp(jnp.sum(lv[0:1] * lv[1:2], axis=-1, keepdims=True))
           - jnp.exp(jnp.sum(lv[2:3] * lv[3:4], axis=-1, keepdims=True)) + lam_init)
    outs = []
    for hh in range(2):
        o = acc_ref[2 * hh] / l_ref[2 * hh] - lam * (acc_ref[2 * hh + 1] / l_ref[2 * hh + 1])
        ms = jnp.mean(o * o, axis=0, keepdims=True)
        outs.append(o * lax.rsqrt(ms + NORM_EPS) * ng_ref[...] * (1.0 - lam_init))
    o_ref[0] = jnp.concatenate(outs, axis=0).T


def _diff_call(lam_vecs, dq, dk, dvt, ng, batch, seq, lam_init):
    nt = seq // TILE
    pairs = DIFF_HEADS // 2
    q = dq.reshape(batch, seq, W_DIFF)
    k = dk.reshape(batch, nt, TILE, W_DIFF)
    vt = dvt.reshape(batch, nt, W_DIFF, TILE)
    return pl.pallas_call(
        functools.partial(_diff_kernel, lam_init=lam_init),
        grid=(batch, pairs, nt),
        in_specs=[
            pl.BlockSpec(lam_vecs.shape, lambda b, p, i: (0, 0)),
            pl.BlockSpec((1, TILE, LANES), lambda b, p, i: (b, i, p)),
            pl.BlockSpec((1, nt, TILE, LANES), lambda b, p, i: (b, 0, 0, p)),
            pl.BlockSpec((1, nt, LANES, TILE), lambda b, p, i: (b, 0, p, 0)),
            pl.BlockSpec(ng.shape, lambda b, p, i: (0, 0)),
        ],
        out_specs=pl.BlockSpec((1, TILE, LANES), lambda b, p, i: (b, i, p)),
        out_shape=jax.ShapeDtypeStruct((batch, seq, W_DIFF), jnp.float32),
        scratch_shapes=[
            pltpu.VMEM((4, DIFF_DV, TILE), jnp.float32),
            pltpu.VMEM((4, 1, TILE), jnp.float32),
            pltpu.VMEM((4, 1, TILE), jnp.float32),
        ],
        compiler_params=pltpu.CompilerParams(
            dimension_semantics=("parallel", "parallel", "arbitrary"), vmem_limit_bytes=VMEM_LIMIT),
        name="diff_attn",
    )(lam_vecs, q, k, vt, ng)


def _dsa_kernel(iq_ref, iwt_ref, qlat_ref, ika_ref, ikb_ref, ckv_ref, ckvt_ref, wuvt_ref, o_ref,
                key_ref, acc_ref, m_ref, l_ref, *, top_k, seq):
    f32, bf16, i32 = jnp.float32, jnp.bfloat16, jnp.int32
    qi = pl.program_id(1)
    n_tiles = qi + 1
    rows = lax.broadcasted_iota(i32, (TILE, TILE), 0)
    rel = rows - lax.broadcasted_iota(i32, (TILE, TILE), 1)
    iq = iq_ref[0]
    iw = iwt_ref[0, 0]

    def fill(j, carry):
        ik_a = ika_ref[0, j]
        ik_b = ikb_ref[0, j]
        score = jnp.zeros((TILE, TILE), f32)
        for hd in range(N_IDX_HEADS):
            iq_pair = iq[:, (hd // 2) * LANES:(hd // 2 + 1) * LANES]
            x = lax.dot_general(ik_a if hd % 2 == 0 else ik_b, iq_pair, _NT, preferred_element_type=f32)
            score = score + jnp.maximum(x, 0.0) * iw[hd:hd + 1, :]
        score = jnp.where(rel <= (qi - j) * TILE, score, NEG)
        score = jnp.where(score == 0.0, 0.0, score)
        bits = pltpu.bitcast(score, i32)
        key_ref[j] = bits ^ ((bits >> 31) & 0x7FFFFFFF)
        return carry

    lax.fori_loop(0, n_tiles, fill, 0)

    n_masked = seq - n_tiles * TILE

    def count(pred):
        def tile_body(j, cnt):
            ind = jnp.where(pred(key_ref[j], j), 1, 0).astype(i32)
            return cnt + jnp.sum(ind.reshape(TILE // 8, 8, TILE), axis=0)
        cnt8 = lax.fori_loop(0, n_tiles, tile_body, jnp.zeros((8, TILE), i32))
        return jnp.sum(cnt8, axis=0, keepdims=True)

    def bit_body(b, t):
        cand = t + jnp.left_shift(jnp.int32(1), 31 - b)
        cnt = count(lambda key, j: key >= cand) + jnp.where(NEG_KEY >= cand, n_masked, 0)
        return jnp.where(cnt >= top_k, cand, t)

    t = lax.fori_loop(0, 32, bit_body, jnp.full((1, TILE), INT_MIN, i32))
    n_gt = count(lambda key, j: key > t) + jnp.where(NEG_KEY > t, n_masked, 0)
    n_eq = count(lambda key, j: key == t)
    room = top_k - n_gt

    def tie_break():
        def pos_body(b, lim):
            cand = lim + jnp.left_shift(jnp.int32(1), 12 - b)
            cnt = count(lambda key, j: (key == t) & (rows + j * TILE < cand))
            return jnp.where(cnt <= room, cand, lim)
        return lax.fori_loop(0, 13, pos_body, jnp.zeros((1, TILE), i32))

    need = jnp.max(jnp.where(n_eq > room, 1, 0)) > 0
    pos_lim = lax.cond(need, tie_break, lambda: jnp.full((1, TILE), 2 * seq, i32))

    m_ref[...] = jnp.full(m_ref.shape, M_INIT, f32)
    l_ref[...] = jnp.zeros(l_ref.shape, f32)
    acc_ref[...] = jnp.zeros(acc_ref.shape, f32)
    qlat = qlat_ref[0]

    def attend(j, carry):
        key = key_ref[j]
        sel = (key > t) | ((key == t) & (rows + j * TILE < pos_lim))
        sel = sel & (rel <= (qi - j) * TILE)
        c_j = ckv_ref[0, j]
        ct_j = ckvt_ref[0, j]
        for hd in range(DSA_HEADS):
            q_h = qlat[:, hd * DSA_LATENT:(hd + 1) * DSA_LATENT]
            s = lax.dot_general(c_j, q_h, _NT, preferred_element_type=f32)
            s = jnp.where(sel, s, MASKED)
            m_old = m_ref[hd]
            m_new = jnp.maximum(m_old, jnp.max(s, axis=0, keepdims=True))
            alpha = jnp.exp2(m_old - m_new)
            p = jnp.exp2(s - m_new)
            l_ref[hd] = alpha * l_ref[hd] + jnp.sum(p, axis=0, keepdims=True)
            acc_ref[hd] = alpha * acc_ref[hd] + jnp.dot(ct_j, p.astype(bf16), preferred_element_type=f32)
            m_ref[hd] = m_new
        return carry

    lax.fori_loop(0, n_tiles, attend, 0)

    outs = []
    for hd in range(DSA_HEADS):
        o_lat = (acc_ref[hd] / l_ref[hd]).astype(bf16)
        outs.append(jnp.dot(wuvt_ref[hd], o_lat, preferred_element_type=f32))
    o_ref[0] = jnp.concatenate(outs, axis=0).T


def _dsa_call(iq, iwt, qlat, ika, ikb, ckv, ckvt, wuvt, batch, seq, top_k):
    nt = seq // TILE
    kv4 = lambda a: a.reshape(batch, nt, TILE, LANES)
    kv_spec = pl.BlockSpec((1, nt, TILE, LANES), lambda b, i: (b, 0, 0, 0))
    return pl.pallas_call(
        functools.partial(_dsa_kernel, top_k=top_k, seq=seq),
        grid=(batch, nt),
        in_specs=[
            pl.BlockSpec((1, TILE, W_IDX), lambda b, i: (b, i, 0)),
            pl.BlockSpec((1, 1, 16, TILE), lambda b, i: (b, i, 0, 0)),
            pl.BlockSpec((1, TILE, W_QLAT), lambda b, i: (b, i, 0)),
            kv_spec, kv_spec, kv_spec,
            pl.BlockSpec((1, nt, DSA_LATENT, TILE), lambda b, i: (b, 0, 0, 0)),
            pl.BlockSpec(wuvt.shape, lambda b, i: (0, 0, 0)),
        ],
        out_specs=pl.BlockSpec((1, TILE, W_DSA), lambda b, i: (b, i, 0)),
        out_shape=jax.ShapeDtypeStruct((batch, seq, W_DSA), jnp.float32),
        scratch_shapes=[
            pltpu.VMEM((nt, TILE, TILE), jnp.int32),
            pltpu.VMEM((DSA_HEADS, DSA_LATENT, TILE), jnp.float32),
            pltpu.VMEM((DSA_HEADS, 1, TILE), jnp.float32),
            pltpu.VMEM((DSA_HEADS, 1, TILE), jnp.float32),
        ],
        compiler_params=pltpu.CompilerParams(
            dimension_semantics=("parallel", "arbitrary"), vmem_limit_bytes=VMEM_LIMIT),
        name="dsa_attn",
    )(iq.reshape(batch, seq, W_IDX), iwt.reshape(batch, nt, 16, TILE), qlat.reshape(batch, seq, W_QLAT),
      kv4(ika), kv4(ikb), kv4(ckv), ckvt.reshape(batch, nt, DSA_LATENT, TILE), wuvt)


def _ret_kernel(q_ref, k_ref, v_ref, dintra_ref, xi_ref, zeta_ref, gmat_ref, bd_ref, ng_ref, o_ref,
                state_ref, *, n_chunks):
    f32, bf16 = jnp.float32, jnp.bfloat16
    lane = lax.broadcasted_iota(jnp.int32, (RET_CHUNK, W_RET), 1)
    head_of_lane = lane // RET_DK
    state_ref[...] = jnp.zeros(state_ref.shape, f32)
    ones_bd = bd_ref[...].astype(bf16)

    def chunk(ci, carry):
        q = q_ref[0, ci]
        k = k_ref[0, ci]
        v = v_ref[0, ci]
        state = state_ref[...]
        inner = jnp.zeros((RET_CHUNK, W_RET), f32)
        for hd in range(RET_HEADS):
            mine = head_of_lane == hd
            att = lax.dot_general(jnp.where(mine, q, jnp.zeros_like(q)), k, _NT,
                                  preferred_element_type=f32) * dintra_ref[hd]
            inner = inner + jnp.dot(att.astype(bf16), jnp.where(mine, v, jnp.zeros_like(v)),
                                    preferred_element_type=f32)
        cross = jnp.dot(q, state.astype(bf16), preferred_element_type=f32) * xi_ref[...]
        kz = (k.astype(f32) * zeta_ref[...]).T.astype(bf16)
        state_ref[...] = state * gmat_ref[...] + jnp.dot(kz, v, preferred_element_type=f32) * bd_ref[...]
        o = inner + cross
        ss = jnp.dot((o * o).astype(bf16), ones_bd, preferred_element_type=f32)
        o_ref[0, ci] = o * lax.rsqrt(ss * (1.0 / RET_DV) + NORM_EPS) * ng_ref[...]
        return carry

    lax.fori_loop(0, n_chunks, chunk, 0)


def _ret_call(rq, rk, rv, tables, ng, batch, seq):
    nc = seq // RET_CHUNK
    dintra, xi, zeta, gmat, bd = tables
    r4 = lambda a: a.reshape(batch, nc, RET_CHUNK, W_RET)
    blk = pl.BlockSpec((1, nc, RET_CHUNK, W_RET), lambda b: (b, 0, 0, 0))
    full = lambda a: pl.BlockSpec(a.shape, lambda b: (0,) * a.ndim)
    out = pl.pallas_call(
        functools.partial(_ret_kernel, n_chunks=nc),
        grid=(batch,),
        in_specs=[blk, blk, blk, full(dintra), full(xi), full(zeta), full(gmat), full(bd), full(ng)],
        out_specs=blk,
        out_shape=jax.ShapeDtypeStruct((batch, nc, RET_CHUNK, W_RET), jnp.float32),
        scratch_shapes=[pltpu.VMEM((W_RET, W_RET), jnp.float32)],
        compiler_params=pltpu.CompilerParams(dimension_semantics=("parallel",),
                                             vmem_limit_bytes=VMEM_LIMIT),
        name="retention",
    )(r4(rq), r4(rk), r4(rv), dintra, xi, zeta, gmat, bd, ng)
    return out.reshape(batch * seq, W_RET)


def _mix_kernel(h_ref, g_ref, oa_ref, ob_ref, oc_ref, w_ref, fg_ref, o_ref, *, final):
    mixed = jnp.concatenate([oa_ref[...], ob_ref[...], oc_ref[...]], axis=1)
    y = (g_ref[...] * mixed).astype(jnp.bfloat16)
    h = h_ref[...] + jnp.dot(y, w_ref[...], preferred_element_type=jnp.float32)
    if final:
        ms = jnp.mean(h * h, axis=-1, keepdims=True)
        h = h * lax.rsqrt(ms + NORM_EPS) * fg_ref[...]
    o_ref[...] = h


def _mix_call(h, g, oa, ob, oc, w, fg, final):
    n = h.shape[0]
    row = lambda width: pl.BlockSpec((TM, width), lambda i: (i, 0))
    full = lambda a: pl.BlockSpec(a.shape, lambda i: (0,) * a.ndim)
    return pl.pallas_call(
        functools.partial(_mix_kernel, final=final),
        grid=(n // TM,),
        in_specs=[row(D_MODEL), row(D_MODEL), row(W_DIFF), row(W_DSA), row(W_RET), full(w), full(fg)],
        out_specs=row(D_MODEL),
        out_shape=jax.ShapeDtypeStruct((n, D_MODEL), jnp.float32),
        compiler_params=pltpu.CompilerParams(dimension_semantics=("parallel",),
                                             vmem_limit_bytes=VMEM_LIMIT),
        name="mix_out",
    )(h, g, oa, ob, oc, w, fg)


def _rotary_tables(seq):
    inv_freq = ROPE_BASE ** (-jnp.arange(RET_DK // 2, dtype=jnp.float32) / (RET_DK // 2))
    ang = jnp.arange(seq, dtype=jnp.float32)[:, None] * inv_freq[None, :]
    cos, sin = jnp.cos(ang), jnp.sin(ang)
    rc = jnp.tile(jnp.concatenate([cos, cos], axis=1), (1, RET_HEADS))
    rs = jnp.tile(jnp.concatenate([-sin, sin], axis=1), (1, RET_HEADS))
    return rc, rs


def _retention_tables():
    c = RET_CHUNK
    log_g = jnp.log(1.0 - 2.0 ** (-5.0 - jnp.arange(RET_HEADS, dtype=jnp.float32)))
    pos = jnp.arange(c, dtype=jnp.float32)
    diff = pos[:, None] - pos[None, :]
    dintra = jnp.where(diff >= 0, jnp.exp(jnp.maximum(diff, 0.0)[None] * log_g[:, None, None]), 0.0)
    xi = jnp.repeat(jnp.exp((pos + 1.0)[:, None] * log_g[None, :]), RET_DK, axis=1)
    zeta = jnp.repeat(jnp.exp((c - 1.0 - pos)[:, None] * log_g[None, :]), RET_DK, axis=1)
    head = jnp.arange(W_RET) // RET_DK
    bd = (head[:, None] == head[None, :]).astype(jnp.float32)
    gmat = bd * jnp.exp(c * log_g)[head][:, None]
    return dintra, xi, zeta, gmat, bd


def _pack_w_in(w):
    sizes = (W_DIFF, W_DIFF, W_DIFF, W_DSA, DSA_LATENT, W_IDX, D_IDX, N_IDX_HEADS, W_RET, W_RET, W_RET, D_MODEL)
    offs = np.concatenate([[0], np.cumsum(sizes)])
    dq, dk, dv, sq, ckv, iq, ik, iw, rq, rk, rv, gate = (w[:, offs[i]:offs[i + 1]] for i in range(len(sizes)))
    pad = jnp.zeros((w.shape[0], LANES - D_IDX), w.dtype)
    main = jnp.concatenate([dq, dk, dv, sq, ckv, iq, ik, pad, rq, rk, rv, gate], axis=1).astype(jnp.bfloat16)
    wiw = jnp.concatenate([iw.T, jnp.zeros((16 - N_IDX_HEADS, w.shape[0]), w.dtype)], axis=0).astype(jnp.bfloat16)
    return main, wiw


def _block_diag_uk(w_uk):
    eye = jnp.eye(DSA_HEADS, dtype=w_uk.dtype)
    return jnp.einsum('hdr,hg->hdgr', w_uk, eye).reshape(W_DSA, W_QLAT).astype(jnp.bfloat16)


def kernel(x, attn_norm, w_in, diff_lambda, diff_norm, kv_norm, w_uk, w_uv, ret_norm, w_out, final_norm):
    batch, seq, d = x.shape
    assert d == D_MODEL and seq % TM == 0 and w_in.shape[0] == DEPTH
    top_k = min(DSA_TOPK_MAX, seq // 4)
    rc, rs = _rotary_tables(seq)
    ret_tables = _retention_tables()
    h = x.reshape(batch * seq, d)
    for layer in range(DEPTH):
        lam_init = 0.8 - 0.6 * math.exp(-0.3 * layer)
        w_main, wiw = _pack_w_in(w_in[layer])
        (dq, dk, dvt, qlat, ckv, ckvt, iq, ika, ikb, iwt, rq, rk, rv, g) = _proj_call(
            h, attn_norm[layer][None, :], w_main, wiw, _block_diag_uk(w_uk[layer]),
            kv_norm[layer][None, :], rc, rs, seq)
        oa = _diff_call(diff_lambda[layer], dq, dk, dvt, diff_norm[layer][:, None], batch, seq, lam_init)
        ob = _dsa_call(iq, iwt, qlat, ika, ikb, ckv, ckvt,
                       jnp.swapaxes(w_uv[layer], 1, 2).astype(jnp.bfloat16), batch, seq, top_k)
        oc = _ret_call(rq, rk, rv, ret_tables, jnp.tile(ret_norm[layer], RET_HEADS)[None, :], batch, seq)
        h = _mix_call(h, g, oa.reshape(batch * seq, W_DIFF), ob.reshape(batch * seq, W_DSA), oc,
                      w_out[layer].astype(jnp.bfloat16), final_norm[None, :], layer == DEPTH - 1)
    return h.reshape(batch, seq, d)
```

```python
import functools
import math

import numpy as np
import jax
import jax.numpy as jnp
from jax import lax
from jax.experimental import pallas as pl
from jax.experimental.pallas import tpu as pltpu

D_MODEL = 1024
DEPTH = 4
DIFF_HEADS = 6
DIFF_DK = 32
DIFF_DV = 64
DSA_HEADS = 6
DSA_DH = 64
DSA_LATENT = 128
N_IDX_HEADS = 8
D_IDX = 64
DSA_TOPK_MAX = 256
RET_HEADS = 4
RET_DK = 64
RET_DV = 64
RET_CHUNK = 128
ROPE_BASE = 10000.0
NORM_EPS = 1e-6
NEG = -1e30
IDX_SCALE = (N_IDX_HEADS ** -0.5) * (D_IDX ** -0.5)
LOG2E = 1.4426950408889634

W_DIFF = DIFF_HEADS * 2 * DIFF_DK
W_DSA = DSA_HEADS * DSA_DH
W_IDX = N_IDX_HEADS * D_IDX
W_RET = RET_HEADS * RET_DK
W_QLAT = DSA_HEADS * DSA_LATENT

C_DQ = 0
C_DK = C_DQ + W_DIFF
C_DV = C_DK + W_DIFF
C_SQ = C_DV + W_DIFF
C_CKV = C_SQ + W_DSA
C_IQ = C_CKV + DSA_LATENT
C_IK = C_IQ + W_IDX
C_RQ = C_IK + 128
C_RK = C_RQ + W_RET
C_RV = C_RK + W_RET
C_GATE = C_RV + W_RET
C_END = C_GATE + D_MODEL

LANES = 128
TILE = 256
TM = 512
VMEM_LIMIT = 56 * 1024 * 1024
MASKED = -3.0e38
M_INIT = -1.0e30
F32_TINY = 1.1754944e-38

_NT = (((1,), (1,)), ((), ()))


def _neg_key():
    bits = int(np.array(NEG, np.float32).view(np.int32))
    return bits ^ ((bits >> 31) & 0x7FFFFFFF)


NEG_KEY = _neg_key()


def _proj_kernel(h_ref, gain_ref, w_ref, wiw_ref, wuk_ref, kvg_ref, rc_ref, rs_ref,
                 dq_ref, dk_ref, dvt_ref, qlat_ref, ckv_ref, ckvt_ref, iq_ref, ika_ref, ikb_ref,
                 iwt_ref, rq_ref, rk_ref, rv_ref, g_ref):
    f32, bf16 = jnp.float32, jnp.bfloat16
    h = h_ref[...]
    ms = jnp.mean(h * h, axis=-1, keepdims=True)
    u = (h * lax.rsqrt(ms + NORM_EPS) * gain_ref[...]).astype(bf16)
    n_sub = h.shape[0] // TILE

    def proj(lo, hi):
        return jnp.dot(u, w_ref[:, lo:hi], preferred_element_type=f32)

    dq_ref[...] = (proj(C_DQ, C_DK) * (DIFF_DK ** -0.5 * LOG2E)).astype(bf16)
    dk_ref[...] = proj(C_DK, C_DV).astype(bf16)
    dv = proj(C_DV, C_SQ)
    for t in range(n_sub):
        dvt_ref[t] = dv[t * TILE:(t + 1) * TILE].T.astype(bf16)

    sq = proj(C_SQ, C_CKV).astype(bf16)
    qlat = jnp.dot(sq, wuk_ref[...], preferred_element_type=f32)
    qlat_ref[...] = (qlat * (DSA_DH ** -0.5 * LOG2E)).astype(bf16)
    ckv = proj(C_CKV, C_IQ)
    cms = jnp.mean(ckv * ckv, axis=-1, keepdims=True)
    c = ckv * lax.rsqrt(cms + NORM_EPS) * kvg_ref[...]
    ckv_ref[...] = c.astype(bf16)
    for t in range(n_sub):
        ckvt_ref[t] = c[t * TILE:(t + 1) * TILE].T.astype(bf16)

    iq_ref[...] = proj(C_IQ, C_IK).astype(bf16)
    ik = proj(C_IK, C_RQ)
    ika_ref[...] = ik.astype(bf16)
    ikb_ref[...] = pltpu.roll(ik, D_IDX, 1).astype(bf16)
    iwt = lax.dot_general(wiw_ref[...], u, _NT, preferred_element_type=f32) * IDX_SCALE
    for t in range(n_sub):
        iwt_ref[t] = iwt[:, t * TILE:(t + 1) * TILE]

    lane = lax.broadcasted_iota(jnp.int32, (h.shape[0], LANES), 1)
    first_half = (lane % RET_DK) < (RET_DK // 2)

    def rotary(x):
        parts = []
        for a in range(W_RET // LANES):
            xa = x[:, a * LANES:(a + 1) * LANES]
            partner = jnp.where(first_half, pltpu.roll(xa, LANES - RET_DK // 2, 1),
                                pltpu.roll(xa, RET_DK // 2, 1))
            sl = slice(a * LANES, (a + 1) * LANES)
            parts.append(xa * rc_ref[:, sl] + partner * rs_ref[:, sl])
        return jnp.concatenate(parts, axis=1)

    rq_ref[...] = rotary(proj(C_RQ, C_RK)).astype(bf16)
    rk_ref[...] = (rotary(proj(C_RK, C_RV)) * (RET_DK ** -0.5)).astype(bf16)
    rv_ref[...] = proj(C_RV, C_GATE).astype(bf16)

    gate = proj(C_GATE, C_END)
    g_ref[...] = gate / (1.0 + jnp.exp(-gate))


def _proj_call(h, gain, w, wiw, wuk, kvg, rc, rs, seq):
    n = h.shape[0]
    nt = n // TILE
    sub = TM // TILE
    pos_blocks = seq // TM
    bf16, f32 = jnp.bfloat16, jnp.float32
    row = lambda width: pl.BlockSpec((TM, width), lambda i: (i, 0))
    full = lambda a: pl.BlockSpec(a.shape, lambda i: (0,) * a.ndim)
    tposed = lambda rows: pl.BlockSpec((sub, rows, TILE), lambda i: (i, 0, 0))
    out_shape = (
        jax.ShapeDtypeStruct((n, W_DIFF), bf16),
        jax.ShapeDtypeStruct((n, W_DIFF), bf16),
        jax.ShapeDtypeStruct((nt, W_DIFF, TILE), bf16),
        jax.ShapeDtypeStruct((n, W_QLAT), bf16),
        jax.ShapeDtypeStruct((n, DSA_LATENT), bf16),
        jax.ShapeDtypeStruct((nt, DSA_LATENT, TILE), bf16),
        jax.ShapeDtypeStruct((n, W_IDX), bf16),
        jax.ShapeDtypeStruct((n, LANES), bf16),
        jax.ShapeDtypeStruct((n, LANES), bf16),
        jax.ShapeDtypeStruct((nt, 16, TILE), f32),
        jax.ShapeDtypeStruct((n, W_RET), bf16),
        jax.ShapeDtypeStruct((n, W_RET), bf16),
        jax.ShapeDtypeStruct((n, W_RET), bf16),
        jax.ShapeDtypeStruct((n, D_MODEL), f32),
    )
    out_specs = (
        row(W_DIFF), row(W_DIFF), tposed(W_DIFF), row(W_QLAT), row(DSA_LATENT), tposed(DSA_LATENT),
        row(W_IDX), row(LANES), row(LANES), tposed(16), row(W_RET), row(W_RET), row(W_RET), row(D_MODEL),
    )
    tab = pl.BlockSpec((TM, W_RET), lambda i: (i % pos_blocks, 0))
    return pl.pallas_call(
        _proj_kernel,
        grid=(n // TM,),
        in_specs=[row(D_MODEL), full(gain), full(w), full(wiw), full(wuk), full(kvg), tab, tab],
        out_specs=out_specs,
        out_shape=out_shape,
        compiler_params=pltpu.CompilerParams(dimension_semantics=("parallel",),
                                             vmem_limit_bytes=VMEM_LIMIT),
        name="proj",
    )(h, gain, w, wiw, wuk, kvg, rc, rs)


def _col_groups(x):
    return x.reshape(x.shape[0] // 8, 8, x.shape[1])


def _tree_sum(xs):
    while len(xs) > 1:
        xs = [xs[i] + xs[i + 1] for i in range(0, len(xs) - 1, 2)] + ([xs[-1]] if len(xs) % 2 else [])
    return xs[0]


def _softmax_pv(n_streams, n_full, scores_fn, values_fn, s_ref, acc_ref):
    f32 = jnp.float32
    acc_ref[...] = jnp.zeros(acc_ref.shape, f32)

    def pass_a(j, diag, m8):
        out = []
        for c, s in enumerate(scores_fn(j, diag)):
            s_ref[c, j] = s
            out.append(jnp.maximum(m8[c], jnp.max(_col_groups(s), axis=0)))
        return tuple(out)

    m8 = tuple(jnp.full((8, TILE), M_INIT, f32) for _ in range(n_streams))
    m8 = lax.fori_loop(0, n_full, lambda j, m: pass_a(j, False, m), m8)
    m8 = pass_a(n_full, True, m8)
    m = [jnp.max(x, axis=0, keepdims=True) for x in m8]

    def pass_b(j, l8):
        out = []
        for c, v in enumerate(values_fn(j)):
            p = jnp.exp2(s_ref[c, j] - m[c])
            out.append(l8[c] + jnp.sum(_col_groups(p), axis=0))
            acc_ref[c] += jnp.dot(v, p.astype(jnp.bfloat16), preferred_element_type=f32)
        return tuple(out)

    l8 = tuple(jnp.zeros((8, TILE), f32) for _ in range(n_streams))
    l8 = lax.fori_loop(0, n_full + 1, pass_b, l8)
    return [jnp.sum(x, axis=0, keepdims=True) for x in l8]


def _diff_kernel(lam_ref, q_ref, k_ref, vt_ref, ng_ref, o_ref, s_ref, acc_ref, *, lam_init):
    f32 = jnp.float32
    n_maps = 2 * DIFF_HEADS
    maps_per_group = LANES // DIFF_DK
    qi = pl.program_id(1)
    q = q_ref[0]
    lane = lax.broadcasted_iota(jnp.int32, (TILE, LANES), 1)
    qm = []
    for c in range(n_maps):
        grp, sub = divmod(c, maps_per_group)
        qg = q[:, grp * LANES:(grp + 1) * LANES]
        qm.append(jnp.where((lane >= DIFF_DK * sub) & (lane < DIFF_DK * (sub + 1)), qg, jnp.zeros_like(qg)))
    visible = (lax.broadcasted_iota(jnp.int32, (TILE, TILE), 0)
               <= lax.broadcasted_iota(jnp.int32, (TILE, TILE), 1))

    def scores(j, diag):
        k_j = k_ref[0, j]
        out = []
        for c in range(n_maps):
            grp = c // maps_per_group
            s = lax.dot_general(k_j[:, grp * LANES:(grp + 1) * LANES], qm[c], _NT,
                                preferred_element_type=f32)
            out.append(jnp.where(visible, s, MASKED) if diag else s)
        return out

    def values(j):
        vt_j = vt_ref[0, j]
        return [vt_j[(c // 2) * DIFF_DV:(c // 2 + 1) * DIFF_DV, :] for c in range(n_maps)]

    l = _softmax_pv(n_maps, qi, scores, values, s_ref, acc_ref)

    lv = lam_ref[...]
    lam = (jnp.exp(jnp.sum(lv[0:1] * lv[1:2], axis=-1, keepdims=True))
           - jnp.exp(jnp.sum(lv[2:3] * lv[3:4], axis=-1, keepdims=True)) + lam_init)
    outs = []
    for hh in range(DIFF_HEADS):
        o = acc_ref[2 * hh] / l[2 * hh] - lam * (acc_ref[2 * hh + 1] / l[2 * hh + 1])
        ms = jnp.mean(o * o, axis=0, keepdims=True)
        outs.append(o * lax.rsqrt(ms + NORM_EPS) * ng_ref[...] * (1.0 - lam_init))
    o_ref[0] = jnp.concatenate(outs, axis=0).T


def _diff_call(lam_vecs, dq, dk, dvt, ng, batch, seq, lam_init):
    nt = seq // TILE
    q = dq.reshape(batch, seq, W_DIFF)
    k = dk.reshape(batch, nt, TILE, W_DIFF)
    vt = dvt.reshape(batch, nt, W_DIFF, TILE)
    return pl.pallas_call(
        functools.partial(_diff_kernel, lam_init=lam_init),
        grid=(batch, nt),
        in_specs=[
            pl.BlockSpec(lam_vecs.shape, lambda b, i: (0, 0)),
            pl.BlockSpec((1, TILE, W_DIFF), lambda b, i: (b, i, 0)),
            pl.BlockSpec((1, nt, TILE, W_DIFF), lambda b, i: (b, 0, 0, 0)),
            pl.BlockSpec((1, nt, W_DIFF, TILE), lambda b, i: (b, 0, 0, 0)),
            pl.BlockSpec(ng.shape, lambda b, i: (0, 0)),
        ],
        out_specs=pl.BlockSpec((1, TILE, W_DIFF), lambda b, i: (b, i, 0)),
        out_shape=jax.ShapeDtypeStruct((batch, seq, W_DIFF), jnp.float32),
        scratch_shapes=[
            pltpu.VMEM((2 * DIFF_HEADS, nt, TILE, TILE), jnp.float32),
            pltpu.VMEM((2 * DIFF_HEADS, DIFF_DV, TILE), jnp.float32),
        ],
        compiler_params=pltpu.CompilerParams(
            dimension_semantics=("parallel", "arbitrary"), vmem_limit_bytes=VMEM_LIMIT),
        name="diff_attn",
    )(lam_vecs, q, k, vt, ng)


def _dsa_kernel(iq_ref, iwt_ref, qlat_ref, ika_ref, ikb_ref, ckv_ref, ckvt_ref, wuvt_ref, o_ref,
                key_ref, top_ref, s_ref, acc_ref, *, top_k, seq):
    f32, bf16, i32 = jnp.float32, jnp.bfloat16, jnp.int32
    qi = pl.program_id(1)
    n_tiles = qi + 1
    rows = lax.broadcasted_iota(i32, (TILE, TILE), 0)
    visible = rows <= lax.broadcasted_iota(i32, (TILE, TILE), 1)
    iq = iq_ref[0]
    iw = iwt_ref[0, 0]

    def fill(j, diag):
        ik_a = ika_ref[0, j]
        ik_b = ikb_ref[0, j]
        score = jnp.zeros((TILE, TILE), f32)
        for hd in range(N_IDX_HEADS):
            iq_pair = iq[:, (hd // 2) * LANES:(hd // 2 + 1) * LANES]
            x = lax.dot_general(ik_a if hd % 2 == 0 else ik_b, iq_pair, _NT, preferred_element_type=f32)
            score = score + jnp.maximum(x, 0.0) * iw[hd:hd + 1, :]
        if diag:
            score = jnp.where(visible, score, NEG)
        score = jnp.where(jnp.abs(score) < F32_TINY, 0.0, score)
        bits = pltpu.bitcast(score, i32)
        key_ref[j] = bits ^ ((bits >> 31) & 0x7FFFFFFF)
        top_ref[j] = pltpu.bitcast(bits & (-65536), f32).astype(bf16)

    def fill_body(j, carry):
        fill(j, False)
        return carry

    lax.fori_loop(0, qi, fill_body, 0)
    fill(qi, True)

    n_masked = seq - n_tiles * TILE

    def count(indicator):
        def tile_body(j, cnt):
            return cnt + jnp.sum(_col_groups(indicator(key_ref[j], j)), axis=0)
        cnt8 = lax.fori_loop(0, n_tiles, tile_body, jnp.zeros((8, TILE), i32))
        return jnp.sum(cnt8, axis=0, keepdims=True)

    one, zero = jnp.ones((), bf16), jnp.zeros((), bf16)

    def top_body(b, t16):
        cand16 = t16 + jnp.left_shift(jnp.int32(1), 15 - b)
        cand_bits = cand16 ^ ((cand16 >> 31) & 0x7FFF)
        cand = pltpu.bitcast(jnp.left_shift(cand_bits, 16), f32).astype(bf16)

        def tile_body(j, cnt):
            ind = jnp.where(top_ref[j] >= cand, one, zero)
            return cnt + _tree_sum([ind[r:r + 16] for r in range(0, TILE, 16)])
        cnt16 = lax.fori_loop(0, n_tiles, tile_body, jnp.zeros((16, TILE), bf16))
        cnt = jnp.sum(cnt16.astype(f32), axis=0, keepdims=True).astype(i32)
        cnt = cnt + jnp.where((NEG_KEY >> 16) >= cand16, n_masked, 0)
        return jnp.where(cnt >= top_k, cand16, t16)

    t16 = lax.fori_loop(0, 16, top_body, jnp.full((1, TILE), -2 ** 15, i32))

    def bit_body(b, t):
        cand = t + jnp.left_shift(jnp.int32(1), 15 - b)
        cnt = count(lambda key, j: jnp.where(key >= cand, 1, 0)) + jnp.where(NEG_KEY >= cand, n_masked, 0)
        return jnp.where(cnt >= top_k, cand, t)

    t = lax.fori_loop(0, 16, bit_body, jnp.left_shift(t16, 16))
    n_gt = count(lambda key, j: jnp.where(key > t, 1, 0)) + jnp.where(NEG_KEY > t, n_masked, 0)
    n_eq = count(lambda key, j: jnp.where(key == t, 1, 0))
    room = top_k - n_gt

    def tie_break():
        def pos_body(b, lim):
            cand = lim + jnp.left_shift(jnp.int32(1), 12 - b)
            cnt = count(lambda key, j: jnp.where(key == t, jnp.where(rows + j * TILE < cand, 1, 0), 0))
            return jnp.where(cnt <= room, cand, lim)
        return lax.fori_loop(0, 13, pos_body, jnp.zeros((1, TILE), i32))

    need = jnp.max(jnp.where(n_eq > room, 1, 0)) > 0
    pos_lim = lax.cond(need, tie_break, lambda: jnp.full((1, TILE), 2 * seq, i32))

    qlat = qlat_ref[0]

    def scores(j, diag):
        key = key_ref[j]
        tied = jnp.where(rows + j * TILE < pos_lim, 0.0, MASKED)
        bias = jnp.where(key > t, 0.0, jnp.where(key == t, tied, MASKED))
        if diag:
            bias = jnp.where(visible, bias, MASKED)
        c_j = ckv_ref[0, j]
        return [lax.dot_general(c_j, qlat[:, hd * DSA_LATENT:(hd + 1) * DSA_LATENT], _NT,
                                preferred_element_type=f32) + bias for hd in range(DSA_HEADS)]

    def values(j):
        return [ckvt_ref[0, j]] * DSA_HEADS

    l = _softmax_pv(DSA_HEADS, qi, scores, values, s_ref, acc_ref)

    outs = []
    for hd in range(DSA_HEADS):
        o_lat = (acc_ref[hd] / l[hd]).astype(bf16)
        outs.append(jnp.dot(wuvt_ref[hd], o_lat, preferred_element_type=f32))
    o_ref[0] = jnp.concatenate(outs, axis=0).T


def _dsa_call(iq, iwt, qlat, ika, ikb, ckv, ckvt, wuvt, batch, seq, top_k):
    nt = seq // TILE
    kv4 = lambda a: a.reshape(batch, nt, TILE, LANES)
    kv_spec = pl.BlockSpec((1, nt, TILE, LANES), lambda b, i: (b, 0, 0, 0))
    return pl.pallas_call(
        functools.partial(_dsa_kernel, top_k=top_k, seq=seq),
        grid=(batch, nt),
        in_specs=[
            pl.BlockSpec((1, TILE, W_IDX), lambda b, i: (b, i, 0)),
            pl.BlockSpec((1, 1, 16, TILE), lambda b, i: (b, i, 0, 0)),
            pl.BlockSpec((1, TILE, W_QLAT), lambda b, i: (b, i, 0)),
            kv_spec, kv_spec, kv_spec,
            pl.BlockSpec((1, nt, DSA_LATENT, TILE), lambda b, i: (b, 0, 0, 0)),
            pl.BlockSpec(wuvt.shape, lambda b, i: (0, 0, 0)),
        ],
        out_specs=pl.BlockSpec((1, TILE, W_DSA), lambda b, i: (b, i, 0)),
        out_shape=jax.ShapeDtypeStruct((batch, seq, W_DSA), jnp.float32),
        scratch_shapes=[
            pltpu.VMEM((nt, TILE, TILE), jnp.int32),
            pltpu.VMEM((nt, TILE, TILE), jnp.bfloat16),
            pltpu.VMEM((DSA_HEADS, nt, TILE, TILE), jnp.float32),
            pltpu.VMEM((DSA_HEADS, DSA_LATENT, TILE), jnp.float32),
        ],
        compiler_params=pltpu.CompilerParams(
            dimension_semantics=("parallel", "arbitrary"), vmem_limit_bytes=VMEM_LIMIT),
        name="dsa_attn",
    )(iq.reshape(batch, seq, W_IDX), iwt.reshape(batch, nt, 16, TILE), qlat.reshape(batch, seq, W_QLAT),
      kv4(ika), kv4(ikb), kv4(ckv), ckvt.reshape(batch, nt, DSA_LATENT, TILE), wuvt)


def _ret_kernel(q_ref, k_ref, v_ref, dintra_ref, xi_ref, zeta_ref, gmat_ref, bd_ref, ng_ref, o_ref,
                state_ref, *, n_chunks):
    f32, bf16 = jnp.float32, jnp.bfloat16
    lane = lax.broadcasted_iota(jnp.int32, (RET_CHUNK, W_RET), 1)
    head_of_lane = lane // RET_DK
    state_ref[...] = jnp.zeros(state_ref.shape, f32)
    ones_bd = bd_ref[...].astype(bf16)

    def chunk(ci, carry):
        q = q_ref[0, ci]
        k = k_ref[0, ci]
        v = v_ref[0, ci]
        state = state_ref[...]
        inner = jnp.zeros((RET_CHUNK, W_RET), f32)
        for hd in range(RET_HEADS):
            mine = head_of_lane == hd
            att = lax.dot_general(jnp.where(mine, q, jnp.zeros_like(q)), k, _NT,
                                  preferred_element_type=f32) * dintra_ref[hd]
            inner = inner + jnp.dot(att.astype(bf16), jnp.where(mine, v, jnp.zeros_like(v)),
                                    preferred_element_type=f32)
        cross = jnp.dot(q, state.astype(bf16), preferred_element_type=f32) * xi_ref[...]
        kz = (k.astype(f32) * zeta_ref[...]).T.astype(bf16)
        state_ref[...] = state * gmat_ref[...] + jnp.dot(kz, v, preferred_element_type=f32) * bd_ref[...]
        o = inner + cross
        ss = jnp.dot((o * o).astype(bf16), ones_bd, preferred_element_type=f32)
        o_ref[0, ci] = o * lax.rsqrt(ss * (1.0 / RET_DV) + NORM_EPS) * ng_ref[...]
        return carry

    lax.fori_loop(0, n_chunks, chunk, 0)


def _ret_call(rq, rk, rv, tables, ng, batch, seq):
    nc = seq // RET_CHUNK
    dintra, xi, zeta, gmat, bd = tables
    r4 = lambda a: a.reshape(batch, nc, RET_CHUNK, W_RET)
    blk = pl.BlockSpec((1, nc, RET_CHUNK, W_RET), lambda b: (b, 0, 0, 0))
    full = lambda a: pl.BlockSpec(a.shape, lambda b: (0,) * a.ndim)
    out = pl.pallas_call(
        functools.partial(_ret_kernel, n_chunks=nc),
        grid=(batch,),
        in_specs=[blk, blk, blk, full(dintra), full(xi), full(zeta), full(gmat), full(bd), full(ng)],
        out_specs=blk,
        out_shape=jax.ShapeDtypeStruct((batch, nc, RET_CHUNK, W_RET), jnp.float32),
        scratch_shapes=[pltpu.VMEM((W_RET, W_RET), jnp.float32)],
        compiler_params=pltpu.CompilerParams(dimension_semantics=("parallel",),
                                             vmem_limit_bytes=VMEM_LIMIT),
        name="retention",
    )(r4(rq), r4(rk), r4(rv), dintra, xi, zeta, gmat, bd, ng)
    return out.reshape(batch * seq, W_RET)


def _mix_kernel(h_ref, g_ref, oa_ref, ob_ref, oc_ref, w_ref, fg_ref, o_ref, *, final):
    mixed = jnp.concatenate([oa_ref[...], ob_ref[...], oc_ref[...]], axis=1)
    y = (g_ref[...] * mixed).astype(jnp.bfloat16)
    h = h_ref[...] + jnp.dot(y, w_ref[...], preferred_element_type=jnp.float32)
    if final:
        ms = jnp.mean(h * h, axis=-1, keepdims=True)
        h = h * lax.rsqrt(ms + NORM_EPS) * fg_ref[...]
    o_ref[...] = h


def _mix_call(h, g, oa, ob, oc, w, fg, final):
    n = h.shape[0]
    row = lambda width: pl.BlockSpec((TM, width), lambda i: (i, 0))
    full = lambda a: pl.BlockSpec(a.shape, lambda i: (0,) * a.ndim)
    return pl.pallas_call(
        functools.partial(_mix_kernel, final=final),
        grid=(n // TM,),
        in_specs=[row(D_MODEL), row(D_MODEL), row(W_DIFF), row(W_DSA), row(W_RET), full(w), full(fg)],
        out_specs=row(D_MODEL),
        out_shape=jax.ShapeDtypeStruct((n, D_MODEL), jnp.float32),
        compiler_params=pltpu.CompilerParams(dimension_semantics=("parallel",),
                                             vmem_limit_bytes=VMEM_LIMIT),
        name="mix_out",
    )(h, g, oa, ob, oc, w, fg)


def _rotary_tables(seq):
    inv_freq = ROPE_BASE ** (-jnp.arange(RET_DK // 2, dtype=jnp.float32) / (RET_DK // 2))
    ang = jnp.arange(seq, dtype=jnp.float32)[:, None] * inv_freq[None, :]
    cos, sin = jnp.cos(ang), jnp.sin(ang)
    rc = jnp.tile(jnp.concatenate([cos, cos], axis=1), (1, RET_HEADS))
    rs = jnp.tile(jnp.concatenate([-sin, sin], axis=1), (1, RET_HEADS))
    return rc, rs


def _retention_tables():
    c = RET_CHUNK
    log_g = jnp.log(1.0 - 2.0 ** (-5.0 - jnp.arange(RET_HEADS, dtype=jnp.float32)))
    pos = jnp.arange(c, dtype=jnp.float32)
    diff = pos[:, None] - pos[None, :]
    dintra = jnp.where(diff >= 0, jnp.exp(jnp.maximum(diff, 0.0)[None] * log_g[:, None, None]), 0.0)
    xi = jnp.repeat(jnp.exp((pos + 1.0)[:, None] * log_g[None, :]), RET_DK, axis=1)
    zeta = jnp.repeat(jnp.exp((c - 1.0 - pos)[:, None] * log_g[None, :]), RET_DK, axis=1)
    head = jnp.arange(W_RET) // RET_DK
    bd = (head[:, None] == head[None, :]).astype(jnp.float32)
    gmat = bd * jnp.exp(c * log_g)[head][:, None]
    return dintra, xi, zeta, gmat, bd


def _pack_w_in(w):
    sizes = (W_DIFF, W_DIFF, W_DIFF, W_DSA, DSA_LATENT, W_IDX, D_IDX, N_IDX_HEADS, W_RET, W_RET, W_RET, D_MODEL)
    offs = np.concatenate([[0], np.cumsum(sizes)])
    dq, dk, dv, sq, ckv, iq, ik, iw, rq, rk, rv, gate = (w[:, offs[i]:offs[i + 1]] for i in range(len(sizes)))
    pad = jnp.zeros((w.shape[0], LANES - D_IDX), w.dtype)
    main = jnp.concatenate([dq, dk, dv, sq, ckv, iq, ik, pad, rq, rk, rv, gate], axis=1).astype(jnp.bfloat16)
    wiw = jnp.concatenate([iw.T, jnp.zeros((16 - N_IDX_HEADS, w.shape[0]), w.dtype)], axis=0).astype(jnp.bfloat16)
    return main, wiw


def _block_diag_uk(w_uk):
    eye = jnp.eye(DSA_HEADS, dtype=w_uk.dtype)
    return jnp.einsum('hdr,hg->hdgr', w_uk, eye).reshape(W_DSA, W_QLAT).astype(jnp.bfloat16)


def kernel(x, attn_norm, w_in, diff_lambda, diff_norm, kv_norm, w_uk, w_uv, ret_norm, w_out, final_norm):
    batch, seq, d = x.shape
    assert d == D_MODEL and seq % TM == 0 and w_in.shape[0] == DEPTH
    top_k = min(DSA_TOPK_MAX, seq // 4)
    rc, rs = _rotary_tables(seq)
    ret_tables = _retention_tables()
    h = x.reshape(batch * seq, d)
    for layer in range(DEPTH):
        lam_init = 0.8 - 0.6 * math.exp(-0.3 * layer)
        w_main, wiw = _pack_w_in(w_in[layer])
        (dq, dk, dvt, qlat, ckv, ckvt, iq, ika, ikb, iwt, rq, rk, rv, g) = _proj_call(
            h, attn_norm[layer][None, :], w_main, wiw, _block_diag_uk(w_uk[layer]),
            kv_norm[layer][None, :], rc, rs, seq)
        oa = _diff_call(diff_lambda[layer], dq, dk, dvt, diff_norm[layer][:, None], batch, seq, lam_init)
        ob = _dsa_call(iq, iwt, qlat, ika, ikb, ckv, ckvt,
                       jnp.swapaxes(w_uv[layer], 1, 2).astype(jnp.bfloat16), batch, seq, top_k)
        oc = _ret_call(rq, rk, rv, ret_tables, jnp.tile(ret_norm[layer], RET_HEADS)[None, :], batch, seq)
        h = _mix_call(h, g, oa.reshape(batch * seq, W_DIFF), ob.reshape(batch * seq, W_DSA), oc,
                      w_out[layer].astype(jnp.bfloat16), final_norm[None, :], layer == DEPTH - 1)
    return h.reshape(batch, seq, d)
```

```python
import functools
import math

import numpy as np
import jax
import jax.numpy as jnp
from jax import lax
from jax.experimental import pallas as pl
from jax.experimental.pallas import tpu as pltpu

D_MODEL = 1024
DEPTH = 4
DIFF_HEADS = 6
DIFF_DK = 32
DIFF_DV = 64
DSA_HEADS = 6
DSA_DH = 64
DSA_LATENT = 128
N_IDX_HEADS = 8
D_IDX = 64
DSA_TOPK_MAX = 256
RET_HEADS = 4
RET_DK = 64
RET_DV = 64
RET_CHUNK = 256
ROPE_BASE = 10000.0
NORM_EPS = 1e-6
NEG = -1e30
IDX_SCALE = (N_IDX_HEADS ** -0.5) * (D_IDX ** -0.5)
LOG2E = 1.4426950408889634

W_DIFF = DIFF_HEADS * 2 * DIFF_DK
W_DSA = DSA_HEADS * DSA_DH
W_IDX = N_IDX_HEADS * D_IDX
W_RET = RET_HEADS * RET_DK
W_QLAT = DSA_HEADS * DSA_LATENT

C_DQ = 0
C_DK = C_DQ + W_DIFF
C_DV = C_DK + W_DIFF
C_SQ = C_DV + W_DIFF
C_CKV = C_SQ + W_DSA
C_IQ = C_CKV + DSA_LATENT
C_IK = C_IQ + W_IDX
C_RQ = C_IK + 128
C_RK = C_RQ + W_RET
C_RV = C_RK + W_RET
C_GATE = C_RV + W_RET
C_END = C_GATE + D_MODEL

LANES = 128
TILE = 256
TM = 512
VMEM_LIMIT = 56 * 1024 * 1024
MASKED = -3.0e38
M_INIT = -1.0e30
F32_TINY = 1.1754944e-38
INT_MIN = -2 ** 31

_NT = (((1,), (1,)), ((), ()))


def _neg_key():
    bits = int(np.array(NEG, np.float32).view(np.int32))
    return bits ^ ((bits >> 31) & 0x7FFFFFFF)


NEG_KEY = _neg_key()


N_MIX_IN = 6
N_PROJ_IN = 7
N_PROJ_OUT = 14


def _layer_kernel(*refs, has_mix, has_proj, final):
    refs = list(refs)
    h_ref = refs.pop(0)
    mix_in = [refs.pop(0) for _ in range(N_MIX_IN)] if has_mix else None
    proj_in = [refs.pop(0) for _ in range(N_PROJ_IN)] if has_proj else None
    h = h_ref[...]
    if has_mix:
        h = _mix_body(h, *mix_in, final=final)
        refs.pop(0)[...] = h
    if has_proj:
        _proj_body(h, *proj_in, *refs)


def _mix_body(h, g_ref, oa_ref, ob_ref, oc_ref, w_ref, fg_ref, *, final):
    mixed = jnp.concatenate([oa_ref[...], ob_ref[...], oc_ref[...]], axis=1)
    y = (g_ref[...].astype(jnp.float32) * mixed.astype(jnp.float32)).astype(jnp.bfloat16)
    h = h + jnp.dot(y, w_ref[...], preferred_element_type=jnp.float32)
    if final:
        ms = jnp.mean(h * h, axis=-1, keepdims=True)
        h = h * lax.rsqrt(ms + NORM_EPS) * fg_ref[...]
    return h


def _proj_body(h, gain_ref, w_ref, wiw_ref, wuk_ref, kvg_ref, rc_ref, rs_ref,
               dq_ref, dk_ref, dvt_ref, qlat_ref, ckv_ref, ckvt_ref, iq_ref, ika_ref, ikb_ref,
               iwt_ref, rq_ref, rk_ref, rv_ref, g_ref):
    f32, bf16 = jnp.float32, jnp.bfloat16
    ms = jnp.mean(h * h, axis=-1, keepdims=True)
    u = (h * lax.rsqrt(ms + NORM_EPS) * gain_ref[...]).astype(bf16)
    n_sub = h.shape[0] // TILE

    def proj(lo, hi):
        return jnp.dot(u, w_ref[:, lo:hi], preferred_element_type=f32)

    dq_ref[...] = (proj(C_DQ, C_DK) * (DIFF_DK ** -0.5 * LOG2E)).astype(bf16)
    dk_ref[...] = proj(C_DK, C_DV).astype(bf16)
    dv = proj(C_DV, C_SQ)
    for t in range(n_sub):
        dvt_ref[t] = dv[t * TILE:(t + 1) * TILE].T.astype(bf16)

    sq = proj(C_SQ, C_CKV).astype(bf16)
    qlat = jnp.dot(sq, wuk_ref[...], preferred_element_type=f32)
    qlat_ref[...] = (qlat * (DSA_DH ** -0.5 * LOG2E)).astype(bf16)
    ckv = proj(C_CKV, C_IQ)
    cms = jnp.mean(ckv * ckv, axis=-1, keepdims=True)
    c = ckv * lax.rsqrt(cms + NORM_EPS) * kvg_ref[...]
    ckv_ref[...] = c.astype(bf16)
    for t in range(n_sub):
        ckvt_ref[t] = c[t * TILE:(t + 1) * TILE].T.astype(bf16)

    iq_ref[...] = proj(C_IQ, C_IK).astype(bf16)
    ik = proj(C_IK, C_RQ)
    ika_ref[...] = ik.astype(bf16)
    ikb_ref[...] = pltpu.roll(ik, D_IDX, 1).astype(bf16)
    iwt = lax.dot_general(wiw_ref[...], u, _NT, preferred_element_type=f32) * IDX_SCALE
    for t in range(n_sub):
        iwt_ref[t] = iwt[:, t * TILE:(t + 1) * TILE]

    lane = lax.broadcasted_iota(jnp.int32, (h.shape[0], LANES), 1)
    first_half = (lane % RET_DK) < (RET_DK // 2)

    def rotary(x):
        parts = []
        for a in range(W_RET // LANES):
            xa = x[:, a * LANES:(a + 1) * LANES]
            partner = jnp.where(first_half, pltpu.roll(xa, LANES - RET_DK // 2, 1),
                                pltpu.roll(xa, RET_DK // 2, 1))
            sl = slice(a * LANES, (a + 1) * LANES)
            parts.append(xa * rc_ref[:, sl] + partner * rs_ref[:, sl])
        return jnp.concatenate(parts, axis=1)

    rq_ref[...] = rotary(proj(C_RQ, C_RK)).astype(bf16)
    rk_ref[...] = (rotary(proj(C_RK, C_RV)) * (RET_DK ** -0.5)).astype(bf16)
    rv_ref[...] = proj(C_RV, C_GATE).astype(bf16)

    gate = proj(C_GATE, C_END)
    g_ref[...] = (gate / (1.0 + jnp.exp(-gate))).astype(bf16)


def _layer_call(h, mix_args, proj_args, seq, final=False):
    n = h.shape[0]
    nt = n // TILE
    sub = TM // TILE
    pos_blocks = seq // TM
    bf16, f32 = jnp.bfloat16, jnp.float32
    row = lambda width: pl.BlockSpec((TM, width), lambda i: (i, 0))
    full = lambda a: pl.BlockSpec(a.shape, lambda i: (0,) * a.ndim, pipeline_mode=pl.Buffered(1))
    tposed = lambda rows: pl.BlockSpec((sub, rows, TILE), lambda i: (i, 0, 0))
    tab = pl.BlockSpec((TM, W_RET), lambda i: (i % pos_blocks, 0))
    args, in_specs, out_shape, out_specs = [h], [row(D_MODEL)], [], []
    if mix_args is not None:
        g, oa, ob, oc, w_out, fg = mix_args
        args += [g, oa, ob, oc, w_out, fg]
        in_specs += [row(D_MODEL), row(W_DIFF), row(W_DSA), row(W_RET), full(w_out), full(fg)]
        out_shape.append(jax.ShapeDtypeStruct((n, D_MODEL), f32))
        out_specs.append(row(D_MODEL))
    if proj_args is not None:
        gain, w, wiw, wuk, kvg, rc, rs = proj_args
        args += [gain, w, wiw, wuk, kvg, rc, rs]
        in_specs += [full(gain), full(w), full(wiw), full(wuk), full(kvg), tab, tab]
        out_shape += _proj_out_shapes(n, nt)
        out_specs += [
            row(W_DIFF), row(W_DIFF), tposed(W_DIFF), row(W_QLAT), row(DSA_LATENT), tposed(DSA_LATENT),
            row(W_IDX), row(LANES), row(LANES), tposed(16), row(W_RET), row(W_RET), row(W_RET), row(D_MODEL),
        ]
    return pl.pallas_call(
        functools.partial(_layer_kernel, has_mix=mix_args is not None, has_proj=proj_args is not None,
                          final=final),
        grid=(n // TM,),
        in_specs=in_specs,
        out_specs=tuple(out_specs),
        out_shape=tuple(out_shape),
        compiler_params=pltpu.CompilerParams(dimension_semantics=("parallel",),
                                             vmem_limit_bytes=VMEM_LIMIT),
        name="layer",
    )(*args)


def _proj_out_shapes(n, nt):
    bf16, f32 = jnp.bfloat16, jnp.float32
    return [
        jax.ShapeDtypeStruct((n, W_DIFF), bf16),
        jax.ShapeDtypeStruct((n, W_DIFF), bf16),
        jax.ShapeDtypeStruct((nt, W_DIFF, TILE), bf16),
        jax.ShapeDtypeStruct((n, W_QLAT), bf16),
        jax.ShapeDtypeStruct((n, DSA_LATENT), bf16),
        jax.ShapeDtypeStruct((nt, DSA_LATENT, TILE), bf16),
        jax.ShapeDtypeStruct((n, W_IDX), bf16),
        jax.ShapeDtypeStruct((n, LANES), bf16),
        jax.ShapeDtypeStruct((n, LANES), bf16),
        jax.ShapeDtypeStruct((nt, 16, TILE), f32),
        jax.ShapeDtypeStruct((n, W_RET), bf16),
        jax.ShapeDtypeStruct((n, W_RET), bf16),
        jax.ShapeDtypeStruct((n, W_RET), bf16),
        jax.ShapeDtypeStruct((n, D_MODEL), bf16),
    ]


def _col_groups(x):
    return x.reshape(x.shape[0] // 8, 8, x.shape[1])


def _tree_sum(xs):
    while len(xs) > 1:
        xs = [xs[i] + xs[i + 1] for i in range(0, len(xs) - 1, 2)] + ([xs[-1]] if len(xs) % 2 else [])
    return xs[0]


def _bit_transpose32(words):
    w = list(words)
    for j, mask in ((16, 0x0000FFFF), (8, 0x00FF00FF), (4, 0x0F0F0F0F), (2, 0x33333333), (1, 0x55555555)):
        for k in range(32):
            if k & j == 0:
                a, b = w[k], w[k + j]
                swap = ((a >> j) ^ b) & mask
                w[k + j] = b ^ swap
                w[k] = a ^ (swap << j)
    return w


def _softmax_pv(n_streams, n_full, scores_fn, values_fn, s_ref, acc_ref):
    f32 = jnp.float32
    acc_ref[...] = jnp.zeros(acc_ref.shape, f32)

    def pass_a(j, diag, m8):
        out = []
        for c, s in enumerate(scores_fn(j, diag)):
            s_ref[c, j] = s
            out.append(jnp.maximum(m8[c], jnp.max(_col_groups(s), axis=0)))
        return tuple(out)

    m8 = tuple(jnp.full((8, TILE), M_INIT, f32) for _ in range(n_streams))
    m8 = lax.fori_loop(0, n_full, lambda j, m: pass_a(j, False, m), m8)
    m8 = pass_a(n_full, True, m8)
    m = [jnp.max(x, axis=0, keepdims=True) for x in m8]

    def pass_b(j, l8):
        out = []
        for c, v in enumerate(values_fn(j)):
            p = jnp.exp2(s_ref[c, j] - m[c])
            out.append(l8[c] + jnp.sum(_col_groups(p), axis=0))
            acc_ref[c] += jnp.dot(v, p.astype(jnp.bfloat16), preferred_element_type=f32)
        return tuple(out)

    l8 = tuple(jnp.zeros((8, TILE), f32) for _ in range(n_streams))
    l8 = lax.fori_loop(0, n_full + 1, pass_b, l8)
    return [jnp.sum(x, axis=0, keepdims=True) for x in l8]


def _diff_kernel(lam_ref, q_ref, k_ref, vt_ref, ng_ref, o_ref, s_ref, acc_ref, *, lam_init):
    f32 = jnp.float32
    n_maps = 2 * DIFF_HEADS
    maps_per_group = LANES // DIFF_DK
    qi = pl.program_id(1)
    q = q_ref[0]
    lane = lax.broadcasted_iota(jnp.int32, (TILE, LANES), 1)
    qm = []
    for c in range(n_maps):
        grp, sub = divmod(c, maps_per_group)
        qg = q[:, grp * LANES:(grp + 1) * LANES]
        qm.append(jnp.where((lane >= DIFF_DK * sub) & (lane < DIFF_DK * (sub + 1)), qg, jnp.zeros_like(qg)))
    visible = (lax.broadcasted_iota(jnp.int32, (TILE, TILE), 0)
               <= lax.broadcasted_iota(jnp.int32, (TILE, TILE), 1))

    def scores(j, diag):
        k_j = k_ref[0, j]
        out = []
        for c in range(n_maps):
            grp = c // maps_per_group
            s = lax.dot_general(k_j[:, grp * LANES:(grp + 1) * LANES], qm[c], _NT,
                                preferred_element_type=f32)
            out.append(jnp.where(visible, s, MASKED) if diag else s)
        return out

    def values(j):
        vt_j = vt_ref[0, j]
        return [vt_j[(c // 2) * DIFF_DV:(c // 2 + 1) * DIFF_DV, :] for c in range(n_maps)]

    l = _softmax_pv(n_maps, qi, scores, values, s_ref, acc_ref)

    lv = lam_ref[...]
    lam = (jnp.exp(jnp.sum(lv[0:1] * lv[1:2], axis=-1, keepdims=True))
           - jnp.exp(jnp.sum(lv[2:3] * lv[3:4], axis=-1, keepdims=True)) + lam_init)
    outs = []
    for hh in range(DIFF_HEADS):
        o = acc_ref[2 * hh] / l[2 * hh] - lam * (acc_ref[2 * hh + 1] / l[2 * hh + 1])
        ms = jnp.mean(o * o, axis=0, keepdims=True)
        outs.append(o * lax.rsqrt(ms + NORM_EPS) * ng_ref[...] * (1.0 - lam_init))
    o_ref[0] = jnp.concatenate(outs, axis=0).T.astype(o_ref.dtype)


def _diff_call(lam_vecs, dq, dk, dvt, ng, batch, seq, lam_init):
    nt = seq // TILE
    q = dq.reshape(batch, seq, W_DIFF)
    k = dk.reshape(batch, nt, TILE, W_DIFF)
    vt = dvt.reshape(batch, nt, W_DIFF, TILE)
    return pl.pallas_call(
        functools.partial(_diff_kernel, lam_init=lam_init),
        grid=(batch, nt),
        in_specs=[
            pl.BlockSpec(lam_vecs.shape, lambda b, i: (0, 0)),
            pl.BlockSpec((1, TILE, W_DIFF), lambda b, i: (b, i, 0)),
            pl.BlockSpec((1, nt, TILE, W_DIFF), lambda b, i: (b, 0, 0, 0)),
            pl.BlockSpec((1, nt, W_DIFF, TILE), lambda b, i: (b, 0, 0, 0)),
            pl.BlockSpec(ng.shape, lambda b, i: (0, 0)),
        ],
        out_specs=pl.BlockSpec((1, TILE, W_DIFF), lambda b, i: (b, i, 0)),
        out_shape=jax.ShapeDtypeStruct((batch, seq, W_DIFF), jnp.bfloat16),
        scratch_shapes=[
            pltpu.VMEM((2 * DIFF_HEADS, nt, TILE, TILE), jnp.float32),
            pltpu.VMEM((2 * DIFF_HEADS, DIFF_DV, TILE), jnp.float32),
        ],
        compiler_params=pltpu.CompilerParams(
            dimension_semantics=("parallel", "arbitrary"), vmem_limit_bytes=VMEM_LIMIT),
        name="diff_attn",
    )(lam_vecs, q, k, vt, ng)


def _dsa_kernel(iq_ref, iwt_ref, qlat_ref, ika_ref, ikb_ref, ckv_ref, ckvt_ref, wuvt_ref, o_ref,
                key_ref, plane_ref, s_ref, acc_ref, *, top_k, seq):
    f32, bf16, i32 = jnp.float32, jnp.bfloat16, jnp.int32
    qi = pl.program_id(1)
    n_tiles = qi + 1
    rows = lax.broadcasted_iota(i32, (TILE, TILE), 0)
    visible = rows <= lax.broadcasted_iota(i32, (TILE, TILE), 1)
    iq = iq_ref[0]
    iw = iwt_ref[0, 0]

    def fill(j, diag):
        ik_a = ika_ref[0, j]
        ik_b = ikb_ref[0, j]
        score = jnp.zeros((TILE, TILE), f32)
        for hd in range(N_IDX_HEADS):
            iq_pair = iq[:, (hd // 2) * LANES:(hd // 2 + 1) * LANES]
            x = lax.dot_general(ik_a if hd % 2 == 0 else ik_b, iq_pair, _NT, preferred_element_type=f32)
            score = score + jnp.maximum(x, 0.0) * iw[hd:hd + 1, :]
        if diag:
            score = jnp.where(visible, score, NEG)
        score = jnp.where(jnp.abs(score) < F32_TINY, 0.0, score)
        bits = pltpu.bitcast(score, i32)
        key = bits ^ ((bits >> 31) & 0x7FFFFFFF)
        key_ref[j] = key
        ukey = key ^ INT_MIN
        for p, plane in enumerate(_bit_transpose32([ukey[8 * i:8 * i + 8, :] for i in range(32)])):
            plane_ref[j, p] = plane

    @pl.when((pl.program_id(0) == 0) & (qi == 0))
    def _():
        plane_ref[...] = jnp.zeros(plane_ref.shape, i32)

    def fill_body(j, carry):
        fill(j, False)
        return carry

    lax.fori_loop(0, qi, fill_body, 0)
    fill(qi, True)

    n_masked = seq - n_tiles * TILE
    n_kv = key_ref.shape[0]

    def count(indicator):
        def tile_body(j, cnt):
            return cnt + jnp.sum(_col_groups(indicator(key_ref[j], j)), axis=0)
        cnt8 = lax.fori_loop(0, n_tiles, tile_body, jnp.zeros((8, TILE), i32))
        return jnp.sum(cnt8, axis=0, keepdims=True)

    def bit_step(b, carry):
        alive, n_gt, neg_alive, t = carry
        p = 31 - b
        planes = [plane_ref[j, p] for j in range(n_kv)]
        ones8 = _tree_sum([lax.population_count(alive[j] & planes[j]) for j in range(n_kv)])
        neg_bit = jnp.right_shift(jnp.int32(NEG_KEY ^ INT_MIN), p) & 1
        n_one = jnp.sum(ones8, axis=0, keepdims=True) + neg_alive * (neg_bit * n_masked)
        take = (n_gt + n_one) >= top_k
        flip = jnp.where(take, 0, -1)
        alive = tuple(alive[j] & (planes[j] ^ flip) for j in range(n_kv))
        neg_alive = neg_alive & jnp.where(take, neg_bit, 1 - neg_bit)
        return (alive, jnp.where(take, n_gt, n_gt + n_one), neg_alive,
                t | jnp.where(take, jnp.left_shift(jnp.int32(1), p), 0))

    alive0 = tuple(jnp.broadcast_to(jnp.where(j <= qi, -1, 0).astype(i32), (8, TILE)) for j in range(n_kv))
    alive, n_gt, _, t = lax.fori_loop(
        0, 32, bit_step, (alive0, jnp.zeros((1, TILE), i32), jnp.ones((1, TILE), i32), jnp.zeros((1, TILE), i32)))
    t = t ^ INT_MIN
    n_eq = jnp.sum(_tree_sum([lax.population_count(a) for a in alive]), axis=0, keepdims=True)
    room = top_k - n_gt

    def tie_break():
        def pos_body(b, lim):
            cand = lim + jnp.left_shift(jnp.int32(1), 12 - b)
            cnt = count(lambda key, j: jnp.where(key == t, jnp.where(rows + j * TILE < cand, 1, 0), 0))
            return jnp.where(cnt <= room, cand, lim)
        return lax.fori_loop(0, 13, pos_body, jnp.zeros((1, TILE), i32))

    need = jnp.max(jnp.where(n_eq > room, 1, 0)) > 0
    pos_lim = lax.cond(need, tie_break, lambda: jnp.full((1, TILE), 2 * seq, i32))

    qlat = qlat_ref[0]

    def scores(j, diag):
        key = key_ref[j]
        tied = jnp.where(rows + j * TILE < pos_lim, 0.0, MASKED)
        bias = jnp.where(key > t, 0.0, jnp.where(key == t, tied, MASKED))
        if diag:
            bias = jnp.where(visible, bias, MASKED)
        c_j = ckv_ref[0, j]
        return [lax.dot_general(c_j, qlat[:, hd * DSA_LATENT:(hd + 1) * DSA_LATENT], _NT,
                                preferred_element_type=f32) + bias for hd in range(DSA_HEADS)]

    def values(j):
        return [ckvt_ref[0, j]] * DSA_HEADS

    l = _softmax_pv(DSA_HEADS, qi, scores, values, s_ref, acc_ref)

    outs = []
    for hd in range(DSA_HEADS):
        o_lat = (acc_ref[hd] / l[hd]).astype(bf16)
        outs.append(jnp.dot(wuvt_ref[hd], o_lat, preferred_element_type=f32))
    o_ref[0] = jnp.concatenate(outs, axis=0).T.astype(o_ref.dtype)


def _dsa_call(iq, iwt, qlat, ika, ikb, ckv, ckvt, wuvt, batch, seq, top_k):
    nt = seq // TILE
    kv4 = lambda a: a.reshape(batch, nt, TILE, LANES)
    kv_spec = pl.BlockSpec((1, nt, TILE, LANES), lambda b, i: (b, 0, 0, 0))
    return pl.pallas_call(
        functools.partial(_dsa_kernel, top_k=top_k, seq=seq),
        grid=(batch, nt),
        in_specs=[
            pl.BlockSpec((1, TILE, W_IDX), lambda b, i: (b, i, 0)),
            pl.BlockSpec((1, 1, 16, TILE), lambda b, i: (b, i, 0, 0)),
            pl.BlockSpec((1, TILE, W_QLAT), lambda b, i: (b, i, 0)),
            kv_spec, kv_spec, kv_spec,
            pl.BlockSpec((1, nt, DSA_LATENT, TILE), lambda b, i: (b, 0, 0, 0)),
            pl.BlockSpec(wuvt.shape, lambda b, i: (0, 0, 0)),
        ],
        out_specs=pl.BlockSpec((1, TILE, W_DSA), lambda b, i: (b, i, 0)),
        out_shape=jax.ShapeDtypeStruct((batch, seq, W_DSA), jnp.bfloat16),
        scratch_shapes=[
            pltpu.VMEM((nt, TILE, TILE), jnp.int32),
            pltpu.VMEM((nt, 32, 8, TILE), jnp.int32),
            pltpu.VMEM((DSA_HEADS, nt, TILE, TILE), jnp.float32),
            pltpu.VMEM((DSA_HEADS, DSA_LATENT, TILE), jnp.float32),
        ],
        compiler_params=pltpu.CompilerParams(
            dimension_semantics=("arbitrary", "arbitrary"), vmem_limit_bytes=VMEM_LIMIT),
        name="dsa_attn",
    )(iq.reshape(batch, seq, W_IDX), iwt.reshape(batch, nt, 16, TILE), qlat.reshape(batch, seq, W_QLAT),
      kv4(ika), kv4(ikb), kv4(ckv), ckvt.reshape(batch, nt, DSA_LATENT, TILE), wuvt)


def _ret_kernel(q_ref, k_ref, v_ref, dintra_ref, xi_ref, zeta_ref, gmat_ref, bd_ref, ng_ref, o_ref,
                state_ref, *, n_chunks):
    f32, bf16 = jnp.float32, jnp.bfloat16
    lane = lax.broadcasted_iota(jnp.int32, (RET_CHUNK, W_RET), 1)
    head_of_lane = lane // RET_DK
    state_ref[...] = jnp.zeros(state_ref.shape, f32)
    ones_bd = bd_ref[...].astype(bf16)

    def chunk(ci, carry):
        q = q_ref[0, ci]
        k = k_ref[0, ci]
        v = v_ref[0, ci]
        state = state_ref[...]
        inner = jnp.zeros((RET_CHUNK, W_RET), f32)
        for hd in range(RET_HEADS):
            mine = head_of_lane == hd
            att = lax.dot_general(jnp.where(mine, q, jnp.zeros_like(q)), k, _NT,
                                  preferred_element_type=f32) * dintra_ref[hd]
            inner = inner + jnp.dot(att.astype(bf16), jnp.where(mine, v, jnp.zeros_like(v)),
                                    preferred_element_type=f32)
        cross = jnp.dot(q, state.astype(bf16), preferred_element_type=f32) * xi_ref[...]
        kz = (k.astype(f32) * zeta_ref[...]).T.astype(bf16)
        state_ref[...] = state * gmat_ref[...] + jnp.dot(kz, v, preferred_element_type=f32) * bd_ref[...]
        o = inner + cross
        ss = jnp.dot((o * o).astype(bf16), ones_bd, preferred_element_type=f32)
        o_ref[0, ci] = (o * lax.rsqrt(ss * (1.0 / RET_DV) + NORM_EPS) * ng_ref[...]).astype(o_ref.dtype)
        return carry

    lax.fori_loop(0, n_chunks, chunk, 0)


def _ret_call(rq, rk, rv, tables, ng, batch, seq):
    nc = seq // RET_CHUNK
    dintra, xi, zeta, gmat, bd = tables
    r4 = lambda a: a.reshape(batch, nc, RET_CHUNK, W_RET)
    blk = pl.BlockSpec((1, nc, RET_CHUNK, W_RET), lambda b: (b, 0, 0, 0))
    full = lambda a: pl.BlockSpec(a.shape, lambda b: (0,) * a.ndim)
    out = pl.pallas_call(
        functools.partial(_ret_kernel, n_chunks=nc),
        grid=(batch,),
        in_specs=[blk, blk, blk, full(dintra), full(xi), full(zeta), full(gmat), full(bd), full(ng)],
        out_specs=blk,
        out_shape=jax.ShapeDtypeStruct((batch, nc, RET_CHUNK, W_RET), jnp.bfloat16),
        scratch_shapes=[pltpu.VMEM((W_RET, W_RET), jnp.float32)],
        compiler_params=pltpu.CompilerParams(dimension_semantics=("parallel",),
                                             vmem_limit_bytes=VMEM_LIMIT),
        name="retention",
    )(r4(rq), r4(rk), r4(rv), dintra, xi, zeta, gmat, bd, ng)
    return out.reshape(batch * seq, W_RET)


def _rotary_tables(seq):
    inv_freq = ROPE_BASE ** (-jnp.arange(RET_DK // 2, dtype=jnp.float32) / (RET_DK // 2))
    ang = jnp.arange(seq, dtype=jnp.float32)[:, None] * inv_freq[None, :]
    cos, sin = jnp.cos(ang), jnp.sin(ang)
    rc = jnp.tile(jnp.concatenate([cos, cos], axis=1), (1, RET_HEADS))
    rs = jnp.tile(jnp.concatenate([-sin, sin], axis=1), (1, RET_HEADS))
    return rc, rs


def _retention_tables():
    c = RET_CHUNK
    log_g = jnp.log(1.0 - 2.0 ** (-5.0 - jnp.arange(RET_HEADS, dtype=jnp.float32)))
    pos = jnp.arange(c, dtype=jnp.float32)
    diff = pos[:, None] - pos[None, :]
    dintra = jnp.where(diff >= 0, jnp.exp(jnp.maximum(diff, 0.0)[None] * log_g[:, None, None]), 0.0)
    xi = jnp.repeat(jnp.exp((pos + 1.0)[:, None] * log_g[None, :]), RET_DK, axis=1)
    zeta = jnp.repeat(jnp.exp((c - 1.0 - pos)[:, None] * log_g[None, :]), RET_DK, axis=1)
    head = jnp.arange(W_RET) // RET_DK
    bd = (head[:, None] == head[None, :]).astype(jnp.float32)
    gmat = bd * jnp.exp(c * log_g)[head][:, None]
    return dintra, xi, zeta, gmat, bd


def _pack_w_in(w):
    sizes = (W_DIFF, W_DIFF, W_DIFF, W_DSA, DSA_LATENT, W_IDX, D_IDX, N_IDX_HEADS, W_RET, W_RET, W_RET, D_MODEL)
    offs = np.concatenate([[0], np.cumsum(sizes)])
    dq, dk, dv, sq, ckv, iq, ik, iw, rq, rk, rv, gate = (w[:, offs[i]:offs[i + 1]] for i in range(len(sizes)))
    pad = jnp.zeros((w.shape[0], LANES - D_IDX), w.dtype)
    main = jnp.concatenate([dq, dk, dv, sq, ckv, iq, ik, pad, rq, rk, rv, gate], axis=1).astype(jnp.bfloat16)
    wiw = jnp.concatenate([iw.T, jnp.zeros((16 - N_IDX_HEADS, w.shape[0]), w.dtype)], axis=0).astype(jnp.bfloat16)
    return main, wiw


def _block_diag_uk(w_uk):
    eye = jnp.eye(DSA_HEADS, dtype=w_uk.dtype)
    return jnp.einsum('hdr,hg->hdgr', w_uk, eye).reshape(W_DSA, W_QLAT).astype(jnp.bfloat16)


def kernel(x, attn_norm, w_in, diff_lambda, diff_norm, kv_norm, w_uk, w_uv, ret_norm, w_out, final_norm):
    batch, seq, d = x.shape
    assert d == D_MODEL and seq % TM == 0 and w_in.shape[0] == DEPTH
    top_k = min(DSA_TOPK_MAX, seq // 4)
    rc, rs = _rotary_tables(seq)
    ret_tables = _retention_tables()
    h = x.reshape(batch * seq, d)
    mix_args = None
    for layer in range(DEPTH):
        lam_init = 0.8 - 0.6 * math.exp(-0.3 * layer)
        w_main, wiw = _pack_w_in(w_in[layer])
        proj_args = (attn_norm[layer][None, :], w_main, wiw, _block_diag_uk(w_uk[layer]),
                     kv_norm[layer][None, :], rc, rs)
        outs = _layer_call(h, mix_args, proj_args, seq)
        if mix_args is not None:
            h, outs = outs[0], outs[1:]
        (dq, dk, dvt, qlat, ckv, ckvt, iq, ika, ikb, iwt, rq, rk, rv, g) = outs
        oa = _diff_call(diff_lambda[layer], dq, dk, dvt, diff_norm[layer][:, None], batch, seq, lam_init)
        ob = _dsa_call(iq, iwt, qlat, ika, ikb, ckv, ckvt,
                       jnp.swapaxes(w_uv[layer], 1, 2).astype(jnp.bfloat16), batch, seq, top_k)
        oc = _ret_call(rq, rk, rv, ret_tables, jnp.tile(ret_norm[layer], RET_HEADS)[None, :], batch, seq)
        mix_args = (g, oa.reshape(batch * seq, W_DIFF), ob.reshape(batch * seq, W_DSA), oc,
                    w_out[layer].astype(jnp.bfloat16), final_norm[None, :])
    (h,) = _layer_call(h, mix_args, None, seq, final=True)
    return h.reshape(batch, seq, d)
```

```python
import functools
import math

import numpy as np
import jax
import jax.numpy as jnp
from jax import lax
from jax.experimental import pallas as pl
from jax.experimental.pallas import tpu as pltpu

D_MODEL = 1024
DEPTH = 4
DIFF_HEADS = 6
DIFF_DK = 32
DIFF_DV = 64
DSA_HEADS = 6
DSA_DH = 64
DSA_LATENT = 128
N_IDX_HEADS = 8
D_IDX = 64
DSA_TOPK_MAX = 256
RET_HEADS = 4
RET_DK = 64
RET_DV = 64
RET_CHUNK = 256
ROPE_BASE = 10000.0
NORM_EPS = 1e-6
NEG = -1e30
IDX_SCALE = (N_IDX_HEADS ** -0.5) * (D_IDX ** -0.5)
LOG2E = 1.4426950408889634

W_DIFF = DIFF_HEADS * 2 * DIFF_DK
W_DSA = DSA_HEADS * DSA_DH
W_IDX = N_IDX_HEADS * D_IDX
W_RET = RET_HEADS * RET_DK
W_QLAT = DSA_HEADS * DSA_LATENT

C_DQ = 0
C_DK = C_DQ + W_DIFF
C_DV = C_DK + W_DIFF
C_SQ = C_DV + W_DIFF
C_CKV = C_SQ + W_DSA
C_IQ = C_CKV + DSA_LATENT
C_IK = C_IQ + W_IDX
C_RQ = C_IK + 128
C_RK = C_RQ + W_RET
C_RV = C_RK + W_RET
C_GATE = C_RV + W_RET
C_END = C_GATE + D_MODEL

LANES = 128
TILE = 256
TM = 512
VMEM_LIMIT = 56 * 1024 * 1024
MASKED = -3.0e38
M_INIT = -1.0e30
L_MIN = 2.0 ** -80
F32_TINY = 1.1754944e-38
INT_MIN = -2 ** 31

_NT = (((1,), (1,)), ((), ()))


def _neg_key():
    bits = int(np.array(NEG, np.float32).view(np.int32))
    return bits ^ ((bits >> 31) & 0x7FFFFFFF)


NEG_KEY = _neg_key()


N_MIX_IN = 6
N_PROJ_IN = 7
N_PROJ_OUT = 14


def _layer_kernel(*refs, has_mix, has_proj, final):
    refs = list(refs)
    h_ref = refs.pop(0)
    mix_in = [refs.pop(0) for _ in range(N_MIX_IN)] if has_mix else None
    proj_in = [refs.pop(0) for _ in range(N_PROJ_IN)] if has_proj else None
    h = h_ref[...]
    if has_mix:
        h = _mix_body(h, *mix_in, final=final)
        refs.pop(0)[...] = h
    if has_proj:
        _proj_body(h, *proj_in, *refs)


def _mix_body(h, g_ref, oa_ref, ob_ref, oc_ref, w_ref, fg_ref, *, final):
    mixed = jnp.concatenate([oa_ref[...], ob_ref[...], oc_ref[...]], axis=1)
    y = (g_ref[...].astype(jnp.float32) * mixed.astype(jnp.float32)).astype(jnp.bfloat16)
    h = h + jnp.dot(y, w_ref[...], preferred_element_type=jnp.float32)
    if final:
        ms = jnp.mean(h * h, axis=-1, keepdims=True)
        h = h * lax.rsqrt(ms + NORM_EPS) * fg_ref[...]
    return h


def _proj_body(h, gain_ref, w_ref, wiw_ref, wuk_ref, kvg_ref, rc_ref, rs_ref,
               dq_ref, dk_ref, dvt_ref, qlat_ref, ckv_ref, ckvt_ref, iq_ref, ika_ref, ikb_ref,
               iwt_ref, rq_ref, rk_ref, rv_ref, g_ref):
    f32, bf16 = jnp.float32, jnp.bfloat16
    ms = jnp.mean(h * h, axis=-1, keepdims=True)
    u = (h * lax.rsqrt(ms + NORM_EPS) * gain_ref[...]).astype(bf16)
    n_sub = h.shape[0] // TILE

    def proj(lo, hi):
        return jnp.dot(u, w_ref[:, lo:hi], preferred_element_type=f32)

    dq_ref[...] = (proj(C_DQ, C_DK) * (DIFF_DK ** -0.5 * LOG2E)).astype(bf16)
    dk_ref[...] = proj(C_DK, C_DV).astype(bf16)
    dv = proj(C_DV, C_SQ)
    for t in range(n_sub):
        dvt_ref[t] = dv[t * TILE:(t + 1) * TILE].T.astype(bf16)

    sq = proj(C_SQ, C_CKV).astype(bf16)
    qlat = jnp.dot(sq, wuk_ref[...], preferred_element_type=f32)
    qlat_ref[...] = (qlat * (DSA_DH ** -0.5 * LOG2E)).astype(bf16)
    ckv = proj(C_CKV, C_IQ)
    cms = jnp.mean(ckv * ckv, axis=-1, keepdims=True)
    c = ckv * lax.rsqrt(cms + NORM_EPS) * kvg_ref[...]
    ckv_ref[...] = c.astype(bf16)
    for t in range(n_sub):
        ckvt_ref[t] = c[t * TILE:(t + 1) * TILE].T.astype(bf16)

    iq_ref[...] = proj(C_IQ, C_IK).astype(bf16)
    ik = proj(C_IK, C_RQ)
    ika_ref[...] = ik.astype(bf16)
    ikb_ref[...] = pltpu.roll(ik, D_IDX, 1).astype(bf16)
    iwt = lax.dot_general(wiw_ref[...], u, _NT, preferred_element_type=f32) * IDX_SCALE
    for t in range(n_sub):
        iwt_ref[t] = iwt[:, t * TILE:(t + 1) * TILE]

    lane = lax.broadcasted_iota(jnp.int32, (h.shape[0], LANES), 1)
    first_half = (lane % RET_DK) < (RET_DK // 2)

    def rotary(x):
        parts = []
        for a in range(W_RET // LANES):
            xa = x[:, a * LANES:(a + 1) * LANES]
            partner = jnp.where(first_half, pltpu.roll(xa, LANES - RET_DK // 2, 1),
                                pltpu.roll(xa, RET_DK // 2, 1))
            sl = slice(a * LANES, (a + 1) * LANES)
            parts.append(xa * rc_ref[:, sl] + partner * rs_ref[:, sl])
        return jnp.concatenate(parts, axis=1)

    rq_ref[...] = rotary(proj(C_RQ, C_RK)).astype(bf16)
    rk_ref[...] = (rotary(proj(C_RK, C_RV)) * (RET_DK ** -0.5)).astype(bf16)
    rv_ref[...] = proj(C_RV, C_GATE).astype(bf16)

    gate = proj(C_GATE, C_END)
    g_ref[...] = (gate / (1.0 + jnp.exp(-gate))).astype(bf16)


def _layer_call(h, mix_args, proj_args, seq, final=False):
    n = h.shape[0]
    nt = n // TILE
    sub = TM // TILE
    pos_blocks = seq // TM
    bf16, f32 = jnp.bfloat16, jnp.float32
    row = lambda width: pl.BlockSpec((TM, width), lambda i: (i, 0))
    full = lambda a: pl.BlockSpec(a.shape, lambda i: (0,) * a.ndim, pipeline_mode=pl.Buffered(1))
    tposed = lambda rows: pl.BlockSpec((sub, rows, TILE), lambda i: (i, 0, 0))
    tab = pl.BlockSpec((TM, W_RET), lambda i: (i % pos_blocks, 0))
    args, in_specs, out_shape, out_specs = [h], [row(D_MODEL)], [], []
    if mix_args is not None:
        g, oa, ob, oc, w_out, fg = mix_args
        args += [g, oa, ob, oc, w_out, fg]
        in_specs += [row(D_MODEL), row(W_DIFF), row(W_DSA), row(W_RET), full(w_out), full(fg)]
        out_shape.append(jax.ShapeDtypeStruct((n, D_MODEL), f32))
        out_specs.append(row(D_MODEL))
    if proj_args is not None:
        gain, w, wiw, wuk, kvg, rc, rs = proj_args
        args += [gain, w, wiw, wuk, kvg, rc, rs]
        in_specs += [full(gain), full(w), full(wiw), full(wuk), full(kvg), tab, tab]
        out_shape += _proj_out_shapes(n, nt)
        out_specs += [
            row(W_DIFF), row(W_DIFF), tposed(W_DIFF), row(W_QLAT), row(DSA_LATENT), tposed(DSA_LATENT),
            row(W_IDX), row(LANES), row(LANES), tposed(16), row(W_RET), row(W_RET), row(W_RET), row(D_MODEL),
        ]
    return pl.pallas_call(
        functools.partial(_layer_kernel, has_mix=mix_args is not None, has_proj=proj_args is not None,
                          final=final),
        grid=(n // TM,),
        in_specs=in_specs,
        out_specs=tuple(out_specs),
        out_shape=tuple(out_shape),
        compiler_params=pltpu.CompilerParams(dimension_semantics=("parallel",),
                                             vmem_limit_bytes=VMEM_LIMIT),
        name="layer",
    )(*args)


def _proj_out_shapes(n, nt):
    bf16, f32 = jnp.bfloat16, jnp.float32
    return [
        jax.ShapeDtypeStruct((n, W_DIFF), bf16),
        jax.ShapeDtypeStruct((n, W_DIFF), bf16),
        jax.ShapeDtypeStruct((nt, W_DIFF, TILE), bf16),
        jax.ShapeDtypeStruct((n, W_QLAT), bf16),
        jax.ShapeDtypeStruct((n, DSA_LATENT), bf16),
        jax.ShapeDtypeStruct((nt, DSA_LATENT, TILE), bf16),
        jax.ShapeDtypeStruct((n, W_IDX), bf16),
        jax.ShapeDtypeStruct((n, LANES), bf16),
        jax.ShapeDtypeStruct((n, LANES), bf16),
        jax.ShapeDtypeStruct((nt, 16, TILE), f32),
        jax.ShapeDtypeStruct((n, W_RET), bf16),
        jax.ShapeDtypeStruct((n, W_RET), bf16),
        jax.ShapeDtypeStruct((n, W_RET), bf16),
        jax.ShapeDtypeStruct((n, D_MODEL), bf16),
    ]


def _col_groups(x):
    return x.reshape(x.shape[0] // 8, 8, x.shape[1])


def _tree_sum(xs):
    while len(xs) > 1:
        xs = [xs[i] + xs[i + 1] for i in range(0, len(xs) - 1, 2)] + ([xs[-1]] if len(xs) % 2 else [])
    return xs[0]


def _bit_transpose32(words):
    w = list(words)
    for j, mask in ((16, 0x0000FFFF), (8, 0x00FF00FF), (4, 0x0F0F0F0F), (2, 0x33333333), (1, 0x55555555)):
        for k in range(32):
            if k & j == 0:
                a, b = w[k], w[k + j]
                swap = ((a >> j) ^ b) & mask
                w[k + j] = b ^ swap
                w[k] = a ^ (swap << j)
    return w


def _group_rows(width, group):
    row = lax.broadcasted_iota(jnp.int32, (16, width), 0)
    lane = lax.broadcasted_iota(jnp.int32, (16, width), 1)
    return jnp.where((lane >= row * group) & (lane < (row + 1) * group), 1.0, 0.0).astype(jnp.bfloat16)


def _sq_norms_t(x, group_rows):
    xf = x.astype(jnp.float32)
    return lax.dot_general(group_rows, (xf * xf).astype(jnp.bfloat16), _NT, preferred_element_type=jnp.float32)


def _softmax_pv(n_streams, n_full, scores_fn, values_fn, bound, s_ref, acc_ref):
    f32 = jnp.float32
    acc_ref[...] = jnp.zeros(acc_ref.shape, f32)

    def one_pass(j, diag, l8):
        out = []
        vs = values_fn(j)
        for c, s in enumerate(scores_fn(j, diag)):
            p = jnp.exp2(s - bound[c:c + 1, :])
            out.append(l8[c] + jnp.sum(_col_groups(p), axis=0))
            acc_ref[c] += jnp.dot(vs[c], p.astype(jnp.bfloat16), preferred_element_type=f32)
        return tuple(out)

    l8 = tuple(jnp.zeros((8, TILE), f32) for _ in range(n_streams))
    l8 = lax.fori_loop(0, n_full // 2, lambda i, l: one_pass(2 * i + 1, False, one_pass(2 * i, False, l)), l8)
    l8 = lax.cond(n_full % 2 == 1, lambda l: one_pass(n_full - 1, False, l), lambda l: l, l8)
    l8 = one_pass(n_full, True, l8)
    fast_l = [jnp.sum(x, axis=0, keepdims=True) for x in l8]
    smallest = fast_l[0]
    for x in fast_l[1:]:
        smallest = jnp.minimum(smallest, x)
    trusted = jnp.min(jnp.where(smallest >= L_MIN, 1, 0)) > 0
    return lax.cond(trusted, lambda: fast_l,
                    lambda: _softmax_pv_exact(n_streams, n_full, scores_fn, values_fn, s_ref, acc_ref))


def _softmax_pv_exact(n_streams, n_full, scores_fn, values_fn, s_ref, acc_ref):
    f32 = jnp.float32
    acc_ref[...] = jnp.zeros(acc_ref.shape, f32)

    def pass_a(j, diag, m8):
        out = []
        for c, s in enumerate(scores_fn(j, diag)):
            s_ref[c, j] = s
            out.append(jnp.maximum(m8[c], jnp.max(_col_groups(s), axis=0)))
        return tuple(out)

    m8 = tuple(jnp.full((8, TILE), M_INIT, f32) for _ in range(n_streams))
    m8 = lax.fori_loop(0, n_full, lambda j, m: pass_a(j, False, m), m8)
    m8 = pass_a(n_full, True, m8)
    m = [jnp.max(x, axis=0, keepdims=True) for x in m8]

    def pass_b(j, l8):
        out = []
        for c, v in enumerate(values_fn(j)):
            p = jnp.exp2(s_ref[c, j] - m[c])
            out.append(l8[c] + jnp.sum(_col_groups(p), axis=0))
            acc_ref[c] += jnp.dot(v, p.astype(jnp.bfloat16), preferred_element_type=f32)
        return tuple(out)

    l8 = tuple(jnp.zeros((8, TILE), f32) for _ in range(n_streams))
    l8 = lax.fori_loop(0, n_full + 1, pass_b, l8)
    return [jnp.sum(x, axis=0, keepdims=True) for x in l8]


def _diff_kernel(lam_ref, q_ref, k_ref, vt_ref, ng_ref, o_ref, s_ref, acc_ref, kmax_ref, *, lam_init):
    f32 = jnp.float32
    n_maps = 2 * DIFF_HEADS
    maps_per_group = LANES // DIFF_DK
    qi = pl.program_id(1)
    q = q_ref[0]
    lane = lax.broadcasted_iota(jnp.int32, (TILE, LANES), 1)
    qm = []
    for c in range(n_maps):
        grp, sub = divmod(c, maps_per_group)
        qg = q[:, grp * LANES:(grp + 1) * LANES]
        qm.append(jnp.where((lane >= DIFF_DK * sub) & (lane < DIFF_DK * (sub + 1)), qg, jnp.zeros_like(qg)))
    visible = (lax.broadcasted_iota(jnp.int32, (TILE, TILE), 0)
               <= lax.broadcasted_iota(jnp.int32, (TILE, TILE), 1))

    def scores(j, diag):
        k_j = k_ref[0, j]
        out = []
        for c in range(n_maps):
            grp = c // maps_per_group
            s = lax.dot_general(k_j[:, grp * LANES:(grp + 1) * LANES], qm[c], _NT,
                                preferred_element_type=f32)
            out.append(jnp.where(visible, s, MASKED) if diag else s)
        return out

    def values(j):
        vt_j = vt_ref[0, j]
        return [vt_j[(c // 2) * DIFF_DV:(c // 2 + 1) * DIFF_DV, :] for c in range(n_maps)]

    groups = _group_rows(W_DIFF, DIFF_DK)

    @pl.when(qi == 0)
    def _():
        def tile_max(j, mx):
            return jnp.maximum(mx, jnp.max(_sq_norms_t(k_ref[0, j], groups), axis=1, keepdims=True))
        mx = lax.fori_loop(0, k_ref.shape[1], tile_max, jnp.zeros((16, 1), f32))
        kmax_ref[...] = jnp.broadcast_to(mx, kmax_ref.shape)

    bound = jnp.sqrt(_sq_norms_t(q, groups) * kmax_ref[:, 0:1])

    l = _softmax_pv(n_maps, qi, scores, values, bound, s_ref, acc_ref)

    lv = lam_ref[...]
    lam = (jnp.exp(jnp.sum(lv[0:1] * lv[1:2], axis=-1, keepdims=True))
           - jnp.exp(jnp.sum(lv[2:3] * lv[3:4], axis=-1, keepdims=True)) + lam_init)
    outs = []
    for hh in range(DIFF_HEADS):
        o = acc_ref[2 * hh] / l[2 * hh] - lam * (acc_ref[2 * hh + 1] / l[2 * hh + 1])
        ms = jnp.mean(o * o, axis=0, keepdims=True)
        outs.append(o * lax.rsqrt(ms + NORM_EPS) * ng_ref[...] * (1.0 - lam_init))
    o_ref[0] = jnp.concatenate(outs, axis=0).T.astype(o_ref.dtype)


def _diff_call(lam_vecs, dq, dk, dvt, ng, batch, seq, lam_init):
    nt = seq // TILE
    q = dq.reshape(batch, seq, W_DIFF)
    k = dk.reshape(batch, nt, TILE, W_DIFF)
    vt = dvt.reshape(batch, nt, W_DIFF, TILE)
    return pl.pallas_call(
        functools.partial(_diff_kernel, lam_init=lam_init),
        grid=(batch, nt),
        in_specs=[
            pl.BlockSpec(lam_vecs.shape, lambda b, i: (0, 0)),
            pl.BlockSpec((1, TILE, W_DIFF), lambda b, i: (b, i, 0)),
            pl.BlockSpec((1, nt, TILE, W_DIFF), lambda b, i: (b, 0, 0, 0)),
            pl.BlockSpec((1, nt, W_DIFF, TILE), lambda b, i: (b, 0, 0, 0)),
            pl.BlockSpec(ng.shape, lambda b, i: (0, 0)),
        ],
        out_specs=pl.BlockSpec((1, TILE, W_DIFF), lambda b, i: (b, i, 0)),
        out_shape=jax.ShapeDtypeStruct((batch, seq, W_DIFF), jnp.bfloat16),
        scratch_shapes=[
            pltpu.VMEM((2 * DIFF_HEADS, nt, TILE, TILE), jnp.float32),
            pltpu.VMEM((2 * DIFF_HEADS, DIFF_DV, TILE), jnp.float32),
            pltpu.VMEM((16, LANES), jnp.float32),
        ],
        compiler_params=pltpu.CompilerParams(
            dimension_semantics=("arbitrary", "arbitrary"), vmem_limit_bytes=VMEM_LIMIT),
        name="diff_attn",
    )(lam_vecs, q, k, vt, ng)


def _dsa_kernel(iq_ref, iwt_ref, qlat_ref, ika_ref, ikb_ref, ckv_ref, ckvt_ref, wuvt_ref, o_ref,
                key_ref, plane_ref, s_ref, acc_ref, cmax_ref, *, top_k, seq):
    f32, bf16, i32 = jnp.float32, jnp.bfloat16, jnp.int32
    qi = pl.program_id(1)
    n_tiles = qi + 1
    rows = lax.broadcasted_iota(i32, (TILE, TILE), 0)
    visible = rows <= lax.broadcasted_iota(i32, (TILE, TILE), 1)
    iq = iq_ref[0]
    iw = iwt_ref[0, 0]

    def fill(j, diag):
        ik_a = ika_ref[0, j]
        ik_b = ikb_ref[0, j]
        score = jnp.zeros((TILE, TILE), f32)
        for hd in range(N_IDX_HEADS):
            iq_pair = iq[:, (hd // 2) * LANES:(hd // 2 + 1) * LANES]
            x = lax.dot_general(ik_a if hd % 2 == 0 else ik_b, iq_pair, _NT, preferred_element_type=f32)
            score = score + jnp.maximum(x, 0.0) * iw[hd:hd + 1, :]
        if diag:
            score = jnp.where(visible, score, NEG)
        score = jnp.where(jnp.abs(score) < F32_TINY, 0.0, score)
        bits = pltpu.bitcast(score, i32)
        key = bits ^ ((bits >> 31) & 0x7FFFFFFF)
        key_ref[j] = key
        ukey = key ^ INT_MIN
        for p, plane in enumerate(_bit_transpose32([ukey[8 * i:8 * i + 8, :] for i in range(32)])):
            plane_ref[j, p] = plane

    @pl.when((pl.program_id(0) == 0) & (qi == 0))
    def _():
        plane_ref[...] = jnp.zeros(plane_ref.shape, i32)

    def fill_body(j, carry):
        fill(j, False)
        return carry

    lax.fori_loop(0, qi, fill_body, 0)
    fill(qi, True)

    n_masked = seq - n_tiles * TILE
    n_kv = key_ref.shape[0]

    def count(indicator):
        def tile_body(j, cnt):
            return cnt + jnp.sum(_col_groups(indicator(key_ref[j], j)), axis=0)
        cnt8 = lax.fori_loop(0, n_tiles, tile_body, jnp.zeros((8, TILE), i32))
        return jnp.sum(cnt8, axis=0, keepdims=True)

    def bit_step(b, carry):
        alive, n_gt, neg_alive, t = carry
        p = 31 - b
        planes = [plane_ref[j, p] for j in range(n_kv)]
        ones8 = _tree_sum([lax.population_count(alive[j] & planes[j]) for j in range(n_kv)])
        neg_bit = jnp.right_shift(jnp.int32(NEG_KEY ^ INT_MIN), p) & 1
        n_one = jnp.sum(ones8, axis=0, keepdims=True) + neg_alive * (neg_bit * n_masked)
        take = (n_gt + n_one) >= top_k
        flip = jnp.where(take, 0, -1)
        alive = tuple(alive[j] & (planes[j] ^ flip) for j in range(n_kv))
        neg_alive = neg_alive & jnp.where(take, neg_bit, 1 - neg_bit)
        return (alive, jnp.where(take, n_gt, n_gt + n_one), neg_alive,
                t | jnp.where(take, jnp.left_shift(jnp.int32(1), p), 0))

    alive0 = tuple(jnp.broadcast_to(jnp.where(j <= qi, -1, 0).astype(i32), (8, TILE)) for j in range(n_kv))
    alive, n_gt, _, t = lax.fori_loop(
        0, 32, bit_step, (alive0, jnp.zeros((1, TILE), i32), jnp.ones((1, TILE), i32), jnp.zeros((1, TILE), i32)))
    t = t ^ INT_MIN
    n_eq = jnp.sum(_tree_sum([lax.population_count(a) for a in alive]), axis=0, keepdims=True)
    room = top_k - n_gt

    def tie_break():
        def pos_body(b, lim):
            cand = lim + jnp.left_shift(jnp.int32(1), 12 - b)
            cnt = count(lambda key, j: jnp.where(key == t, jnp.where(rows + j * TILE < cand, 1, 0), 0))
            return jnp.where(cnt <= room, cand, lim)
        return lax.fori_loop(0, 13, pos_body, jnp.zeros((1, TILE), i32))

    need = jnp.max(jnp.where(n_eq > room, 1, 0)) > 0
    pos_lim = lax.cond(need, tie_break, lambda: jnp.full((1, TILE), 2 * seq, i32))

    qlat = qlat_ref[0]

    def scores(j, diag):
        key = key_ref[j]
        tied = jnp.where(rows + j * TILE < pos_lim, 0.0, MASKED)
        bias = jnp.where(key > t, 0.0, jnp.where(key == t, tied, MASKED))
        if diag:
            bias = jnp.where(visible, bias, MASKED)
        c_j = ckv_ref[0, j]
        return [lax.dot_general(c_j, qlat[:, hd * DSA_LATENT:(hd + 1) * DSA_LATENT], _NT,
                                preferred_element_type=f32) + bias for hd in range(DSA_HEADS)]

    def values(j):
        return [ckvt_ref[0, j]] * DSA_HEADS

    @pl.when(qi == 0)
    def _():
        ones = jnp.ones((16, DSA_LATENT), bf16)

        def tile_max(j, mx):
            return jnp.maximum(mx, jnp.max(_sq_norms_t(ckv_ref[0, j], ones), axis=1, keepdims=True))
        mx = lax.fori_loop(0, n_kv, tile_max, jnp.zeros((16, 1), f32))
        cmax_ref[...] = jnp.broadcast_to(mx, cmax_ref.shape)

    bound = jnp.sqrt(_sq_norms_t(qlat, _group_rows(W_QLAT, DSA_LATENT)) * cmax_ref[:, 0:1])

    l = _softmax_pv(DSA_HEADS, qi, scores, values, bound, s_ref, acc_ref)

    outs = []
    for hd in range(DSA_HEADS):
        o_lat = (acc_ref[hd] / l[hd]).astype(bf16)
        outs.append(jnp.dot(wuvt_ref[hd], o_lat, preferred_element_type=f32))
    o_ref[0] = jnp.concatenate(outs, axis=0).T.astype(o_ref.dtype)


def _dsa_call(iq, iwt, qlat, ika, ikb, ckv, ckvt, wuvt, batch, seq, top_k):
    nt = seq // TILE
    kv4 = lambda a: a.reshape(batch, nt, TILE, LANES)
    kv_spec = pl.BlockSpec((1, nt, TILE, LANES), lambda b, i: (b, 0, 0, 0))
    return pl.pallas_call(
        functools.partial(_dsa_kernel, top_k=top_k, seq=seq),
        grid=(batch, nt),
        in_specs=[
            pl.BlockSpec((1, TILE, W_IDX), lambda b, i: (b, i, 0)),
            pl.BlockSpec((1, 1, 16, TILE), lambda b, i: (b, i, 0, 0)),
            pl.BlockSpec((1, TILE, W_QLAT), lambda b, i: (b, i, 0)),
            kv_spec, kv_spec, kv_spec,
            pl.BlockSpec((1, nt, DSA_LATENT, TILE), lambda b, i: (b, 0, 0, 0)),
            pl.BlockSpec(wuvt.shape, lambda b, i: (0, 0, 0)),
        ],
        out_specs=pl.BlockSpec((1, TILE, W_DSA), lambda b, i: (b, i, 0)),
        out_shape=jax.ShapeDtypeStruct((batch, seq, W_DSA), jnp.bfloat16),
        scratch_shapes=[
            pltpu.VMEM((nt, TILE, TILE), jnp.int32),
            pltpu.VMEM((nt, 32, 8, TILE), jnp.int32),
            pltpu.VMEM((DSA_HEADS, nt, TILE, TILE), jnp.float32),
            pltpu.VMEM((DSA_HEADS, DSA_LATENT, TILE), jnp.float32),
            pltpu.VMEM((16, LANES), jnp.float32),
        ],
        compiler_params=pltpu.CompilerParams(
            dimension_semantics=("arbitrary", "arbitrary"), vmem_limit_bytes=VMEM_LIMIT),
        name="dsa_attn",
    )(iq.reshape(batch, seq, W_IDX), iwt.reshape(batch, nt, 16, TILE), qlat.reshape(batch, seq, W_QLAT),
      kv4(ika), kv4(ikb), kv4(ckv), ckvt.reshape(batch, nt, DSA_LATENT, TILE), wuvt)


def _ret_kernel(q_ref, k_ref, v_ref, dintra_ref, xi_ref, zeta_ref, gmat_ref, bd_ref, ng_ref, o_ref,
                state_ref, *, n_chunks):
    f32, bf16 = jnp.float32, jnp.bfloat16
    lane = lax.broadcasted_iota(jnp.int32, (RET_CHUNK, W_RET), 1)
    head_of_lane = lane // RET_DK
    state_ref[...] = jnp.zeros(state_ref.shape, f32)
    ones_bd = bd_ref[...].astype(bf16)

    def chunk(ci, carry):
        q = q_ref[0, ci]
        k = k_ref[0, ci]
        v = v_ref[0, ci]
        state = state_ref[...]
        inner = jnp.zeros((RET_CHUNK, W_RET), f32)
        for hd in range(RET_HEADS):
            mine = head_of_lane == hd
            att = lax.dot_general(jnp.where(mine, q, jnp.zeros_like(q)), k, _NT,
                                  preferred_element_type=f32) * dintra_ref[hd]
            inner = inner + jnp.dot(att.astype(bf16), jnp.where(mine, v, jnp.zeros_like(v)),
                                    preferred_element_type=f32)
        cross = jnp.dot(q, state.astype(bf16), preferred_element_type=f32) * xi_ref[...]
        kz = (k.astype(f32) * zeta_ref[...]).T.astype(bf16)
        state_ref[...] = state * gmat_ref[...] + jnp.dot(kz, v, preferred_element_type=f32) * bd_ref[...]
        o = inner + cross
        ss = jnp.dot((o * o).astype(bf16), ones_bd, preferred_element_type=f32)
        o_ref[0, ci] = (o * lax.rsqrt(ss * (1.0 / RET_DV) + NORM_EPS) * ng_ref[...]).astype(o_ref.dtype)
        return carry

    lax.fori_loop(0, n_chunks, chunk, 0)


def _ret_call(rq, rk, rv, tables, ng, batch, seq):
    nc = seq // RET_CHUNK
    dintra, xi, zeta, gmat, bd = tables
    r4 = lambda a: a.reshape(batch, nc, RET_CHUNK, W_RET)
    blk = pl.BlockSpec((1, nc, RET_CHUNK, W_RET), lambda b: (b, 0, 0, 0))
    full = lambda a: pl.BlockSpec(a.shape, lambda b: (0,) * a.ndim)
    out = pl.pallas_call(
        functools.partial(_ret_kernel, n_chunks=nc),
        grid=(batch,),
        in_specs=[blk, blk, blk, full(dintra), full(xi), full(zeta), full(gmat), full(bd), full(ng)],
        out_specs=blk,
        out_shape=jax.ShapeDtypeStruct((batch, nc, RET_CHUNK, W_RET), jnp.bfloat16),
        scratch_shapes=[pltpu.VMEM((W_RET, W_RET), jnp.float32)],
        compiler_params=pltpu.CompilerParams(dimension_semantics=("parallel",),
                                             vmem_limit_bytes=VMEM_LIMIT),
        name="retention",
    )(r4(rq), r4(rk), r4(rv), dintra, xi, zeta, gmat, bd, ng)
    return out.reshape(batch * seq, W_RET)


def _rotary_tables(seq):
    inv_freq = ROPE_BASE ** (-jnp.arange(RET_DK // 2, dtype=jnp.float32) / (RET_DK // 2))
    ang = jnp.arange(seq, dtype=jnp.float32)[:, None] * inv_freq[None, :]
    cos, sin = jnp.cos(ang), jnp.sin(ang)
    rc = jnp.tile(jnp.concatenate([cos, cos], axis=1), (1, RET_HEADS))
    rs = jnp.tile(jnp.concatenate([-sin, sin], axis=1), (1, RET_HEADS))
    return rc, rs


def _retention_tables():
    c = RET_CHUNK
    log_g = jnp.log(1.0 - 2.0 ** (-5.0 - jnp.arange(RET_HEADS, dtype=jnp.float32)))
    pos = jnp.arange(c, dtype=jnp.float32)
    diff = pos[:, None] - pos[None, :]
    dintra = jnp.where(diff >= 0, jnp.exp(jnp.maximum(diff, 0.0)[None] * log_g[:, None, None]), 0.0)
    xi = jnp.repeat(jnp.exp((pos + 1.0)[:, None] * log_g[None, :]), RET_DK, axis=1)
    zeta = jnp.repeat(jnp.exp((c - 1.0 - pos)[:, None] * log_g[None, :]), RET_DK, axis=1)
    head = jnp.arange(W_RET) // RET_DK
    bd = (head[:, None] == head[None, :]).astype(jnp.float32)
    gmat = bd * jnp.exp(c * log_g)[head][:, None]
    return dintra, xi, zeta, gmat, bd


def _pack_w_in(w):
    sizes = (W_DIFF, W_DIFF, W_DIFF, W_DSA, DSA_LATENT, W_IDX, D_IDX, N_IDX_HEADS, W_RET, W_RET, W_RET, D_MODEL)
    offs = np.concatenate([[0], np.cumsum(sizes)])
    dq, dk, dv, sq, ckv, iq, ik, iw, rq, rk, rv, gate = (w[:, offs[i]:offs[i + 1]] for i in range(len(sizes)))
    pad = jnp.zeros((w.shape[0], LANES - D_IDX), w.dtype)
    main = jnp.concatenate([dq, dk, dv, sq, ckv, iq, ik, pad, rq, rk, rv, gate], axis=1).astype(jnp.bfloat16)
    wiw = jnp.concatenate([iw.T, jnp.zeros((16 - N_IDX_HEADS, w.shape[0]), w.dtype)], axis=0).astype(jnp.bfloat16)
    return main, wiw


def _block_diag_uk(w_uk):
    eye = jnp.eye(DSA_HEADS, dtype=w_uk.dtype)
    return jnp.einsum('hdr,hg->hdgr', w_uk, eye).reshape(W_DSA, W_QLAT).astype(jnp.bfloat16)


def kernel(x, attn_norm, w_in, diff_lambda, diff_norm, kv_norm, w_uk, w_uv, ret_norm, w_out, final_norm):
    batch, seq, d = x.shape
    assert d == D_MODEL and seq % TM == 0 and w_in.shape[0] == DEPTH
    top_k = min(DSA_TOPK_MAX, seq // 4)
    rc, rs = _rotary_tables(seq)
    ret_tables = _retention_tables()
    h = x.reshape(batch * seq, d)
    mix_args = None
    for layer in range(DEPTH):
        lam_init = 0.8 - 0.6 * math.exp(-0.3 * layer)
        w_main, wiw = _pack_w_in(w_in[layer])
        proj_args = (attn_norm[layer][None, :], w_main, wiw, _block_diag_uk(w_uk[layer]),
                     kv_norm[layer][None, :], rc, rs)
        outs = _layer_call(h, mix_args, proj_args, seq)
        if mix_args is not None:
            h, outs = outs[0], outs[1:]
        (dq, dk, dvt, qlat, ckv, ckvt, iq, ika, ikb, iwt, rq, rk, rv, g) = outs
        oa = _diff_call(diff_lambda[layer], dq, dk, dvt, diff_norm[layer][:, None], batch, seq, lam_init)
        ob = _dsa_call(iq, iwt, qlat, ika, ikb, ckv, ckvt,
                       jnp.swapaxes(w_uv[layer], 1, 2).astype(jnp.bfloat16), batch, seq, top_k)
        oc = _ret_call(rq, rk, rv, ret_tables, jnp.tile(ret_norm[layer], RET_HEADS)[None, :], batch, seq)
        mix_args = (g, oa.reshape(batch * seq, W_DIFF), ob.reshape(batch * seq, W_DSA), oc,
                    w_out[layer].astype(jnp.bfloat16), final_norm[None, :])
    (h,) = _layer_call(h, mix_args, None, seq, final=True)
    return h.reshape(batch, seq, d)
```

```python
import functools
import math

import numpy as np
import jax
import jax.numpy as jnp
from jax import lax
from jax.experimental import pallas as pl
from jax.experimental.pallas import tpu as pltpu

D_MODEL = 1024
DEPTH = 4
DIFF_HEADS = 6
DIFF_DK = 32
DIFF_DV = 64
DSA_HEADS = 6
DSA_DH = 64
DSA_LATENT = 128
N_IDX_HEADS = 8
D_IDX = 64
DSA_TOPK_MAX = 256
RET_HEADS = 4
RET_DK = 64
RET_DV = 64
RET_CHUNK = 256
ROPE_BASE = 10000.0
NORM_EPS = 1e-6
NEG = -1e30
IDX_SCALE = (N_IDX_HEADS ** -0.5) * (D_IDX ** -0.5)
LOG2E = 1.4426950408889634

W_DIFF = DIFF_HEADS * 2 * DIFF_DK
W_DSA = DSA_HEADS * DSA_DH
W_IDX = N_IDX_HEADS * D_IDX
W_RET = RET_HEADS * RET_DK
W_QLAT = DSA_HEADS * DSA_LATENT

C_DQ = 0
C_DK = C_DQ + W_DIFF
C_DV = C_DK + W_DIFF
C_SQ = C_DV + W_DIFF
C_CKV = C_SQ + W_DSA
C_IQ = C_CKV + DSA_LATENT
C_IK = C_IQ + W_IDX
C_RQ = C_IK + 128
C_RK = C_RQ + W_RET
C_RV = C_RK + W_RET
C_GATE = C_RV + W_RET
C_END = C_GATE + D_MODEL

LANES = 128
TILE = 256
TM = 512
VMEM_LIMIT = 56 * 1024 * 1024
MASKED = -3.0e38
M_INIT = -1.0e30
L_MIN = 2.0 ** -80
F32_TINY = 1.1754944e-38
INT_MIN = -2 ** 31

_NT = (((1,), (1,)), ((), ()))


def _neg_key():
    bits = int(np.array(NEG, np.float32).view(np.int32))
    return bits ^ ((bits >> 31) & 0x7FFFFFFF)


NEG_KEY = _neg_key()


N_MIX_IN = 6
N_PROJ_IN = 7
N_PROJ_OUT = 14


def _layer_kernel(*refs, has_mix, has_proj, final):
    refs = list(refs)
    h_ref = refs.pop(0)
    mix_in = [refs.pop(0) for _ in range(N_MIX_IN)] if has_mix else None
    proj_in = [refs.pop(0) for _ in range(N_PROJ_IN)] if has_proj else None
    h = h_ref[...]
    if has_mix:
        h = _mix_body(h, *mix_in, final=final)
        refs.pop(0)[...] = h
    if has_proj:
        _proj_body(h, *proj_in, *refs)


def _mix_body(h, g_ref, oa_ref, ob_ref, oc_ref, w_ref, fg_ref, *, final):
    mixed = jnp.concatenate([oa_ref[...], ob_ref[...], oc_ref[...]], axis=1)
    y = (g_ref[...].astype(jnp.float32) * mixed.astype(jnp.float32)).astype(jnp.bfloat16)
    h = h + jnp.dot(y, w_ref[...], preferred_element_type=jnp.float32)
    if final:
        ms = jnp.mean(h * h, axis=-1, keepdims=True)
        h = h * lax.rsqrt(ms + NORM_EPS) * fg_ref[...]
    return h


def _proj_body(h, gain_ref, w_ref, wiw_ref, wuk_ref, kvg_ref, rc_ref, rs_ref,
               dq_ref, dk_ref, dvt_ref, qlat_ref, ckv_ref, ckvt_ref, iq_ref, ika_ref, ikb_ref,
               iwt_ref, rq_ref, rk_ref, rv_ref, g_ref):
    f32, bf16 = jnp.float32, jnp.bfloat16
    ms = jnp.mean(h * h, axis=-1, keepdims=True)
    u = (h * lax.rsqrt(ms + NORM_EPS) * gain_ref[...]).astype(bf16)
    n_sub = h.shape[0] // TILE

    def proj(lo, hi):
        return jnp.dot(u, w_ref[:, lo:hi], preferred_element_type=f32)

    dq_ref[...] = (proj(C_DQ, C_DK) * (DIFF_DK ** -0.5 * LOG2E)).astype(bf16)
    dk_ref[...] = proj(C_DK, C_DV).astype(bf16)
    dv = proj(C_DV, C_SQ)
    for t in range(n_sub):
        dvt_ref[t] = dv[t * TILE:(t + 1) * TILE].T.astype(bf16)

    sq = proj(C_SQ, C_CKV).astype(bf16)
    qlat = jnp.dot(sq, wuk_ref[...], preferred_element_type=f32)
    qlat_ref[...] = (qlat * (DSA_DH ** -0.5 * LOG2E)).astype(bf16)
    ckv = proj(C_CKV, C_IQ)
    cms = jnp.mean(ckv * ckv, axis=-1, keepdims=True)
    c = ckv * lax.rsqrt(cms + NORM_EPS) * kvg_ref[...]
    ckv_ref[...] = c.astype(bf16)
    for t in range(n_sub):
        ckvt_ref[t] = c[t * TILE:(t + 1) * TILE].T.astype(bf16)

    iq_ref[...] = proj(C_IQ, C_IK).astype(bf16)
    ik = proj(C_IK, C_RQ)
    ika_ref[...] = ik.astype(bf16)
    ikb_ref[...] = pltpu.roll(ik, D_IDX, 1).astype(bf16)
    iwt = lax.dot_general(wiw_ref[...], u, _NT, preferred_element_type=f32) * IDX_SCALE
    for t in range(n_sub):
        iwt_ref[t] = iwt[:, t * TILE:(t + 1) * TILE]

    lane = lax.broadcasted_iota(jnp.int32, (h.shape[0], LANES), 1)
    first_half = (lane % RET_DK) < (RET_DK // 2)

    def rotary(x):
        parts = []
        for a in range(W_RET // LANES):
            xa = x[:, a * LANES:(a + 1) * LANES]
            partner = jnp.where(first_half, pltpu.roll(xa, LANES - RET_DK // 2, 1),
                                pltpu.roll(xa, RET_DK // 2, 1))
            sl = slice(a * LANES, (a + 1) * LANES)
            parts.append(xa * rc_ref[:, sl] + partner * rs_ref[:, sl])
        return jnp.concatenate(parts, axis=1)

    rq_ref[...] = rotary(proj(C_RQ, C_RK)).astype(bf16)
    rk_ref[...] = (rotary(proj(C_RK, C_RV)) * (RET_DK ** -0.5)).astype(bf16)
    rv_ref[...] = proj(C_RV, C_GATE).astype(bf16)

    gate = proj(C_GATE, C_END)
    g_ref[...] = (gate / (1.0 + jnp.exp(-gate))).astype(bf16)


def _layer_call(h, mix_args, proj_args, seq, final=False):
    n = h.shape[0]
    nt = n // TILE
    sub = TM // TILE
    pos_blocks = seq // TM
    bf16, f32 = jnp.bfloat16, jnp.float32
    row = lambda width: pl.BlockSpec((TM, width), lambda i: (i, 0))
    full = lambda a: pl.BlockSpec(a.shape, lambda i: (0,) * a.ndim, pipeline_mode=pl.Buffered(1))
    tposed = lambda rows: pl.BlockSpec((sub, rows, TILE), lambda i: (i, 0, 0))
    tab = pl.BlockSpec((TM, W_RET), lambda i: (i % pos_blocks, 0))
    args, in_specs, out_shape, out_specs = [h], [row(D_MODEL)], [], []
    if mix_args is not None:
        g, oa, ob, oc, w_out, fg = mix_args
        args += [g, oa, ob, oc, w_out, fg]
        in_specs += [row(D_MODEL), row(W_DIFF), row(W_DSA), row(W_RET), full(w_out), full(fg)]
        out_shape.append(jax.ShapeDtypeStruct((n, D_MODEL), f32))
        out_specs.append(row(D_MODEL))
    if proj_args is not None:
        gain, w, wiw, wuk, kvg, rc, rs = proj_args
        args += [gain, w, wiw, wuk, kvg, rc, rs]
        in_specs += [full(gain), full(w), full(wiw), full(wuk), full(kvg), tab, tab]
        out_shape += _proj_out_shapes(n, nt)
        out_specs += [
            row(W_DIFF), row(W_DIFF), tposed(W_DIFF), row(W_QLAT), row(DSA_LATENT), tposed(DSA_LATENT),
            row(W_IDX), row(LANES), row(LANES), tposed(16), row(W_RET), row(W_RET), row(W_RET), row(D_MODEL),
        ]
    return pl.pallas_call(
        functools.partial(_layer_kernel, has_mix=mix_args is not None, has_proj=proj_args is not None,
                          final=final),
        grid=(n // TM,),
        in_specs=in_specs,
        out_specs=tuple(out_specs),
        out_shape=tuple(out_shape),
        compiler_params=pltpu.CompilerParams(dimension_semantics=("parallel",),
                                             vmem_limit_bytes=VMEM_LIMIT),
        name="layer",
    )(*args)


def _proj_out_shapes(n, nt):
    bf16, f32 = jnp.bfloat16, jnp.float32
    return [
        jax.ShapeDtypeStruct((n, W_DIFF), bf16),
        jax.ShapeDtypeStruct((n, W_DIFF), bf16),
        jax.ShapeDtypeStruct((nt, W_DIFF, TILE), bf16),
        jax.ShapeDtypeStruct((n, W_QLAT), bf16),
        jax.ShapeDtypeStruct((n, DSA_LATENT), bf16),
        jax.ShapeDtypeStruct((nt, DSA_LATENT, TILE), bf16),
        jax.ShapeDtypeStruct((n, W_IDX), bf16),
        jax.ShapeDtypeStruct((n, LANES), bf16),
        jax.ShapeDtypeStruct((n, LANES), bf16),
        jax.ShapeDtypeStruct((nt, 16, TILE), f32),
        jax.ShapeDtypeStruct((n, W_RET), bf16),
        jax.ShapeDtypeStruct((n, W_RET), bf16),
        jax.ShapeDtypeStruct((n, W_RET), bf16),
        jax.ShapeDtypeStruct((n, D_MODEL), bf16),
    ]


def _col_groups(x):
    return x.reshape(x.shape[0] // 8, 8, x.shape[1])


def _tree_sum(xs):
    while len(xs) > 1:
        xs = [xs[i] + xs[i + 1] for i in range(0, len(xs) - 1, 2)] + ([xs[-1]] if len(xs) % 2 else [])
    return xs[0]


def _bit_transpose32(words):
    w = list(words)
    for j, mask in ((16, 0x0000FFFF), (8, 0x00FF00FF), (4, 0x0F0F0F0F), (2, 0x33333333), (1, 0x55555555)):
        for k in range(32):
            if k & j == 0:
                a, b = w[k], w[k + j]
                swap = ((a >> j) ^ b) & mask
                w[k + j] = b ^ swap
                w[k] = a ^ (swap << j)
    return w


def _group_rows(width, group):
    row = lax.broadcasted_iota(jnp.int32, (16, width), 0)
    lane = lax.broadcasted_iota(jnp.int32, (16, width), 1)
    return jnp.where((lane >= row * group) & (lane < (row + 1) * group), 1.0, 0.0).astype(jnp.bfloat16)


def _sq_norms_t(x, group_rows):
    xf = x.astype(jnp.float32)
    return lax.dot_general(group_rows, (xf * xf).astype(jnp.bfloat16), _NT, preferred_element_type=jnp.float32)


def _softmax_pv(n_streams, n_full, scores_fn, values_fn, bound, s_ref, acc_ref):
    f32 = jnp.float32
    acc_ref[...] = jnp.zeros(acc_ref.shape, f32)

    def one_pass(j, diag, l8):
        out = []
        vs = values_fn(j)
        for c, s in enumerate(scores_fn(j, diag)):
            p = jnp.exp2(s - bound[c:c + 1, :])
            out.append(l8[c] + jnp.sum(_col_groups(p), axis=0))
            acc_ref[c] += jnp.dot(vs[c], p.astype(jnp.bfloat16), preferred_element_type=f32)
        return tuple(out)

    l8 = tuple(jnp.zeros((8, TILE), f32) for _ in range(n_streams))
    l8 = lax.fori_loop(0, n_full // 2, lambda i, l: one_pass(2 * i + 1, False, one_pass(2 * i, False, l)), l8)
    l8 = lax.cond(n_full % 2 == 1, lambda l: one_pass(n_full - 1, False, l), lambda l: l, l8)
    l8 = one_pass(n_full, True, l8)
    fast_l = [jnp.sum(x, axis=0, keepdims=True) for x in l8]
    smallest = fast_l[0]
    for x in fast_l[1:]:
        smallest = jnp.minimum(smallest, x)
    trusted = jnp.min(jnp.where(smallest >= L_MIN, 1, 0)) > 0
    return lax.cond(trusted, lambda: fast_l,
                    lambda: _softmax_pv_exact(n_streams, n_full, scores_fn, values_fn, s_ref, acc_ref))


def _softmax_pv_exact(n_streams, n_full, scores_fn, values_fn, s_ref, acc_ref):
    f32 = jnp.float32
    acc_ref[...] = jnp.zeros(acc_ref.shape, f32)

    def pass_a(j, diag, m8):
        out = []
        for c, s in enumerate(scores_fn(j, diag)):
            s_ref[c, j] = s
            out.append(jnp.maximum(m8[c], jnp.max(_col_groups(s), axis=0)))
        return tuple(out)

    m8 = tuple(jnp.full((8, TILE), M_INIT, f32) for _ in range(n_streams))
    m8 = lax.fori_loop(0, n_full, lambda j, m: pass_a(j, False, m), m8)
    m8 = pass_a(n_full, True, m8)
    m = [jnp.max(x, axis=0, keepdims=True) for x in m8]

    def pass_b(j, l8):
        out = []
        for c, v in enumerate(values_fn(j)):
            p = jnp.exp2(s_ref[c, j] - m[c])
            out.append(l8[c] + jnp.sum(_col_groups(p), axis=0))
            acc_ref[c] += jnp.dot(v, p.astype(jnp.bfloat16), preferred_element_type=f32)
        return tuple(out)

    l8 = tuple(jnp.zeros((8, TILE), f32) for _ in range(n_streams))
    l8 = lax.fori_loop(0, n_full + 1, pass_b, l8)
    return [jnp.sum(x, axis=0, keepdims=True) for x in l8]


def _diff_kernel(lam_ref, q_ref, k_ref, vt_ref, ng_ref, o_ref, s_ref, acc_ref, kmax_ref, *, lam_init):
    f32 = jnp.float32
    n_maps = 2 * DIFF_HEADS
    maps_per_group = LANES // DIFF_DK
    qi = pl.program_id(1)
    q = q_ref[0]
    lane = lax.broadcasted_iota(jnp.int32, (TILE, LANES), 1)
    qm = []
    for c in range(n_maps):
        grp, sub = divmod(c, maps_per_group)
        qg = q[:, grp * LANES:(grp + 1) * LANES]
        qm.append(jnp.where((lane >= DIFF_DK * sub) & (lane < DIFF_DK * (sub + 1)), qg, jnp.zeros_like(qg)))
    visible = (lax.broadcasted_iota(jnp.int32, (TILE, TILE), 0)
               <= lax.broadcasted_iota(jnp.int32, (TILE, TILE), 1))

    def scores(j, diag):
        k_j = k_ref[0, j]
        out = []
        for c in range(n_maps):
            grp = c // maps_per_group
            s = lax.dot_general(k_j[:, grp * LANES:(grp + 1) * LANES], qm[c], _NT,
                                preferred_element_type=f32)
            out.append(jnp.where(visible, s, MASKED) if diag else s)
        return out

    def values(j):
        vt_j = vt_ref[0, j]
        return [vt_j[(c // 2) * DIFF_DV:(c // 2 + 1) * DIFF_DV, :] for c in range(n_maps)]

    groups = _group_rows(W_DIFF, DIFF_DK)

    @pl.when(qi == 0)
    def _():
        def tile_max(j, mx):
            return jnp.maximum(mx, jnp.max(_sq_norms_t(k_ref[0, j], groups), axis=1, keepdims=True))
        mx = lax.fori_loop(0, k_ref.shape[1], tile_max, jnp.zeros((16, 1), f32))
        kmax_ref[...] = jnp.broadcast_to(mx, kmax_ref.shape)

    bound = jnp.sqrt(_sq_norms_t(q, groups) * kmax_ref[:, 0:1])

    l = _softmax_pv(n_maps, qi, scores, values, bound, s_ref, acc_ref)

    lv = lam_ref[...]
    lam = (jnp.exp(jnp.sum(lv[0:1] * lv[1:2], axis=-1, keepdims=True))
           - jnp.exp(jnp.sum(lv[2:3] * lv[3:4], axis=-1, keepdims=True)) + lam_init)
    outs = []
    for hh in range(DIFF_HEADS):
        o = acc_ref[2 * hh] / l[2 * hh] - lam * (acc_ref[2 * hh + 1] / l[2 * hh + 1])
        ms = jnp.mean(o * o, axis=0, keepdims=True)
        outs.append(o * lax.rsqrt(ms + NORM_EPS) * ng_ref[...] * (1.0 - lam_init))
    o_ref[0] = jnp.concatenate(outs, axis=0).T.astype(o_ref.dtype)


def _diff_call(lam_vecs, dq, dk, dvt, ng, batch, seq, lam_init):
    nt = seq // TILE
    q = dq.reshape(batch, seq, W_DIFF)
    k = dk.reshape(batch, nt, TILE, W_DIFF)
    vt = dvt.reshape(batch, nt, W_DIFF, TILE)
    return pl.pallas_call(
        functools.partial(_diff_kernel, lam_init=lam_init),
        grid=(batch, nt),
        in_specs=[
            pl.BlockSpec(lam_vecs.shape, lambda b, i: (0, 0)),
            pl.BlockSpec((1, TILE, W_DIFF), lambda b, i: (b, i, 0)),
            pl.BlockSpec((1, nt, TILE, W_DIFF), lambda b, i: (b, 0, 0, 0)),
            pl.BlockSpec((1, nt, W_DIFF, TILE), lambda b, i: (b, 0, 0, 0)),
            pl.BlockSpec(ng.shape, lambda b, i: (0, 0)),
        ],
        out_specs=pl.BlockSpec((1, TILE, W_DIFF), lambda b, i: (b, i, 0)),
        out_shape=jax.ShapeDtypeStruct((batch, seq, W_DIFF), jnp.bfloat16),
        scratch_shapes=[
            pltpu.VMEM((2 * DIFF_HEADS, nt, TILE, TILE), jnp.float32),
            pltpu.VMEM((2 * DIFF_HEADS, DIFF_DV, TILE), jnp.float32),
            pltpu.VMEM((16, LANES), jnp.float32),
        ],
        compiler_params=pltpu.CompilerParams(
            dimension_semantics=("arbitrary", "arbitrary"), vmem_limit_bytes=VMEM_LIMIT),
        name="diff_attn",
    )(lam_vecs, q, k, vt, ng)


def _dsa_kernel(iq_ref, iwt_ref, qlat_ref, ika_ref, ikb_ref, ckv_ref, ckvt_ref, wuvt_ref, o_ref,
                key_ref, plane_ref, s_ref, acc_ref, cmax_ref, *, top_k, seq):
    f32, bf16, i32 = jnp.float32, jnp.bfloat16, jnp.int32
    qi = pl.program_id(1)
    n_tiles = qi + 1
    rows = lax.broadcasted_iota(i32, (TILE, TILE), 0)
    visible = rows <= lax.broadcasted_iota(i32, (TILE, TILE), 1)
    iq = iq_ref[0]
    iw = iwt_ref[0, 0]

    def fill(j, diag):
        ik_a = ika_ref[0, j]
        ik_b = ikb_ref[0, j]
        score = jnp.zeros((TILE, TILE), f32)
        for hd in range(N_IDX_HEADS):
            iq_pair = iq[:, (hd // 2) * LANES:(hd // 2 + 1) * LANES]
            x = lax.dot_general(ik_a if hd % 2 == 0 else ik_b, iq_pair, _NT, preferred_element_type=f32)
            score = score + jnp.maximum(x, 0.0) * iw[hd:hd + 1, :]
        if diag:
            score = jnp.where(visible, score, NEG)
        score = jnp.where(jnp.abs(score) < F32_TINY, 0.0, score)
        bits = pltpu.bitcast(score, i32)
        key = bits ^ ((bits >> 31) & 0x7FFFFFFF)
        key_ref[j] = key
        ukey = key ^ INT_MIN
        for p, plane in enumerate(_bit_transpose32([ukey[8 * i:8 * i + 8, :] for i in range(32)])):
            plane_ref[j, p] = plane

    @pl.when((pl.program_id(0) == 0) & (qi == 0))
    def _():
        plane_ref[...] = jnp.zeros(plane_ref.shape, i32)

    def fill_pair(i, carry):
        fill(2 * i, False)
        fill(2 * i + 1, False)
        return carry

    lax.fori_loop(0, qi // 2, fill_pair, 0)

    @pl.when(qi % 2 == 1)
    def _():
        fill(qi - 1, False)

    fill(qi, True)

    n_masked = seq - n_tiles * TILE
    n_kv = key_ref.shape[0]

    def bit_step(b, carry):
        alive, n_gt, neg_alive, t = carry
        p = 31 - b
        planes = [plane_ref[j, p] for j in range(n_kv)]
        ones8 = _tree_sum([lax.population_count(alive[j] & planes[j]) for j in range(n_kv)])
        neg_bit = jnp.right_shift(jnp.int32(NEG_KEY ^ INT_MIN), p) & 1
        n_one = jnp.sum(ones8, axis=0, keepdims=True) + neg_alive * (neg_bit * n_masked)
        take = (n_gt + n_one) >= top_k
        flip = jnp.where(take, 0, -1)
        alive = tuple(alive[j] & (planes[j] ^ flip) for j in range(n_kv))
        neg_alive = neg_alive & jnp.where(take, neg_bit, 1 - neg_bit)
        return (alive, jnp.where(take, n_gt, n_gt + n_one), neg_alive,
                t | jnp.where(take, jnp.left_shift(jnp.int32(1), p), 0))

    alive0 = tuple(jnp.broadcast_to(jnp.where(j <= qi, -1, 0).astype(i32), (8, TILE)) for j in range(n_kv))
    alive, n_gt, _, t = lax.fori_loop(
        0, 32, bit_step, (alive0, jnp.zeros((1, TILE), i32), jnp.ones((1, TILE), i32), jnp.zeros((1, TILE), i32)))
    t = t ^ INT_MIN
    n_eq = jnp.sum(_tree_sum([lax.population_count(a) for a in alive]), axis=0, keepdims=True)
    room = top_k - n_gt

    pos_bits = seq.bit_length()

    def tie_break():
        sublane = lax.broadcasted_iota(i32, (8, TILE), 0)

        def pos_body(b, lim):
            cand = lim + jnp.left_shift(jnp.int32(1), pos_bits - b)
            below8 = []
            for j in range(n_kv):
                n_below = jnp.clip((cand - (TILE * j - 7) - sublane) >> 3, 0, 32)
                mask = jnp.where(n_below >= 32, -1, jnp.left_shift(jnp.int32(1), jnp.minimum(n_below, 31)) - 1)
                below8.append(lax.population_count(alive[j] & mask))
            cnt = jnp.sum(_tree_sum(below8), axis=0, keepdims=True)
            return jnp.where(cnt <= room, cand, lim)
        return lax.fori_loop(0, pos_bits + 1, pos_body, jnp.zeros((1, TILE), i32))

    need = jnp.max(jnp.where(n_eq > room, 1, 0)) > 0
    pos_lim = lax.cond(need, tie_break, lambda: jnp.full((1, TILE), 2 * seq, i32))

    qlat = qlat_ref[0]

    def scores(j, diag):
        key = key_ref[j]
        tied = jnp.where(rows + j * TILE < pos_lim, 0.0, MASKED)
        bias = jnp.where(key > t, 0.0, jnp.where(key == t, tied, MASKED))
        if diag:
            bias = jnp.where(visible, bias, MASKED)
        c_j = ckv_ref[0, j]
        return [lax.dot_general(c_j, qlat[:, hd * DSA_LATENT:(hd + 1) * DSA_LATENT], _NT,
                                preferred_element_type=f32) + bias for hd in range(DSA_HEADS)]

    def values(j):
        return [ckvt_ref[0, j]] * DSA_HEADS

    @pl.when(qi == 0)
    def _():
        ones = jnp.ones((16, DSA_LATENT), bf16)

        def tile_max(j, mx):
            return jnp.maximum(mx, jnp.max(_sq_norms_t(ckv_ref[0, j], ones), axis=1, keepdims=True))
        mx = lax.fori_loop(0, n_kv, tile_max, jnp.zeros((16, 1), f32))
        cmax_ref[...] = jnp.broadcast_to(mx, cmax_ref.shape)

    bound = jnp.sqrt(_sq_norms_t(qlat, _group_rows(W_QLAT, DSA_LATENT)) * cmax_ref[:, 0:1])

    l = _softmax_pv(DSA_HEADS, qi, scores, values, bound, s_ref, acc_ref)

    outs = []
    for hd in range(DSA_HEADS):
        o_lat = (acc_ref[hd] / l[hd]).astype(bf16)
        outs.append(jnp.dot(wuvt_ref[hd], o_lat, preferred_element_type=f32))
    o_ref[0] = jnp.concatenate(outs, axis=0).T.astype(o_ref.dtype)


def _dsa_call(iq, iwt, qlat, ika, ikb, ckv, ckvt, wuvt, batch, seq, top_k):
    nt = seq // TILE
    kv4 = lambda a: a.reshape(batch, nt, TILE, LANES)
    kv_spec = pl.BlockSpec((1, nt, TILE, LANES), lambda b, i: (b, 0, 0, 0))
    return pl.pallas_call(
        functools.partial(_dsa_kernel, top_k=top_k, seq=seq),
        grid=(batch, nt),
        in_specs=[
            pl.BlockSpec((1, TILE, W_IDX), lambda b, i: (b, i, 0)),
            pl.BlockSpec((1, 1, 16, TILE), lambda b, i: (b, i, 0, 0)),
            pl.BlockSpec((1, TILE, W_QLAT), lambda b, i: (b, i, 0)),
            kv_spec, kv_spec, kv_spec,
            pl.BlockSpec((1, nt, DSA_LATENT, TILE), lambda b, i: (b, 0, 0, 0)),
            pl.BlockSpec(wuvt.shape, lambda b, i: (0, 0, 0)),
        ],
        out_specs=pl.BlockSpec((1, TILE, W_DSA), lambda b, i: (b, i, 0)),
        out_shape=jax.ShapeDtypeStruct((batch, seq, W_DSA), jnp.bfloat16),
        scratch_shapes=[
            pltpu.VMEM((nt, TILE, TILE), jnp.int32),
            pltpu.VMEM((nt, 32, 8, TILE), jnp.int32),
            pltpu.VMEM((DSA_HEADS, nt, TILE, TILE), jnp.float32),
            pltpu.VMEM((DSA_HEADS, DSA_LATENT, TILE), jnp.float32),
            pltpu.VMEM((16, LANES), jnp.float32),
        ],
        compiler_params=pltpu.CompilerParams(
            dimension_semantics=("arbitrary", "arbitrary"), vmem_limit_bytes=VMEM_LIMIT),
        name="dsa_attn",
    )(iq.reshape(batch, seq, W_IDX), iwt.reshape(batch, nt, 16, TILE), qlat.reshape(batch, seq, W_QLAT),
      kv4(ika), kv4(ikb), kv4(ckv), ckvt.reshape(batch, nt, DSA_LATENT, TILE), wuvt)


def _ret_kernel(q_ref, k_ref, v_ref, dintra_ref, xi_ref, zeta_ref, gmat_ref, bd_ref, ng_ref, o_ref,
                state_ref, *, n_chunks):
    f32, bf16 = jnp.float32, jnp.bfloat16
    lane = lax.broadcasted_iota(jnp.int32, (RET_CHUNK, W_RET), 1)
    head_of_lane = lane // RET_DK
    state_ref[...] = jnp.zeros(state_ref.shape, f32)
    ones_bd = bd_ref[...].astype(bf16)

    def chunk(ci, carry):
        for r in range(q_ref.shape[0]):
            chunk_row(r, ci)
        return carry

    def chunk_row(r, ci):
        q = q_ref[r, ci]
        k = k_ref[r, ci]
        v = v_ref[r, ci]
        state = state_ref[r]
        inner = jnp.zeros((RET_CHUNK, W_RET), f32)
        for hd in range(RET_HEADS):
            mine = head_of_lane == hd
            att = lax.dot_general(jnp.where(mine, q, jnp.zeros_like(q)), k, _NT,
                                  preferred_element_type=f32) * dintra_ref[hd]
            inner = inner + jnp.dot(att.astype(bf16), jnp.where(mine, v, jnp.zeros_like(v)),
                                    preferred_element_type=f32)
        cross = jnp.dot(q, state.astype(bf16), preferred_element_type=f32) * xi_ref[...]
        kz = (k.astype(f32) * zeta_ref[...]).T.astype(bf16)
        state_ref[r] = state * gmat_ref[...] + jnp.dot(kz, v, preferred_element_type=f32) * bd_ref[...]
        o = inner + cross
        ss = jnp.dot((o * o).astype(bf16), ones_bd, preferred_element_type=f32)
        o_ref[r, ci] = (o * lax.rsqrt(ss * (1.0 / RET_DV) + NORM_EPS) * ng_ref[...]).astype(o_ref.dtype)

    lax.fori_loop(0, n_chunks, chunk, 0)


def _ret_call(rq, rk, rv, tables, ng, batch, seq):
    nc = seq // RET_CHUNK
    rows = 2 if batch % 2 == 0 else 1
    dintra, xi, zeta, gmat, bd = tables
    r4 = lambda a: a.reshape(batch, nc, RET_CHUNK, W_RET)
    blk = pl.BlockSpec((rows, nc, RET_CHUNK, W_RET), lambda b: (b, 0, 0, 0))
    full = lambda a: pl.BlockSpec(a.shape, lambda b: (0,) * a.ndim)
    out = pl.pallas_call(
        functools.partial(_ret_kernel, n_chunks=nc),
        grid=(batch // rows,),
        in_specs=[blk, blk, blk, full(dintra), full(xi), full(zeta), full(gmat), full(bd), full(ng)],
        out_specs=blk,
        out_shape=jax.ShapeDtypeStruct((batch, nc, RET_CHUNK, W_RET), jnp.bfloat16),
        scratch_shapes=[pltpu.VMEM((rows, W_RET, W_RET), jnp.float32)],
        compiler_params=pltpu.CompilerParams(dimension_semantics=("parallel",),
                                             vmem_limit_bytes=VMEM_LIMIT),
        name="retention",
    )(r4(rq), r4(rk), r4(rv), dintra, xi, zeta, gmat, bd, ng)
    return out.reshape(batch * seq, W_RET)


def _rotary_tables(seq):
    inv_freq = ROPE_BASE ** (-jnp.arange(RET_DK // 2, dtype=jnp.float32) / (RET_DK // 2))
    ang = jnp.arange(seq, dtype=jnp.float32)[:, None] * inv_freq[None, :]
    cos, sin = jnp.cos(ang), jnp.sin(ang)
    rc = jnp.tile(jnp.concatenate([cos, cos], axis=1), (1, RET_HEADS))
    rs = jnp.tile(jnp.concatenate([-sin, sin], axis=1), (1, RET_HEADS))
    return rc, rs


def _retention_tables():
    c = RET_CHUNK
    log_g = jnp.log(1.0 - 2.0 ** (-5.0 - jnp.arange(RET_HEADS, dtype=jnp.float32)))
    pos = jnp.arange(c, dtype=jnp.float32)
    diff = pos[:, None] - pos[None, :]
    dintra = jnp.where(diff >= 0, jnp.exp(jnp.maximum(diff, 0.0)[None] * log_g[:, None, None]), 0.0)
    xi = jnp.repeat(jnp.exp((pos + 1.0)[:, None] * log_g[None, :]), RET_DK, axis=1)
    zeta = jnp.repeat(jnp.exp((c - 1.0 - pos)[:, None] * log_g[None, :]), RET_DK, axis=1)
    head = jnp.arange(W_RET) // RET_DK
    bd = (head[:, None] == head[None, :]).astype(jnp.float32)
    gmat = bd * jnp.exp(c * log_g)[head][:, None]
    return dintra, xi, zeta, gmat, bd


def _pack_w_in(w):
    sizes = (W_DIFF, W_DIFF, W_DIFF, W_DSA, DSA_LATENT, W_IDX, D_IDX, N_IDX_HEADS, W_RET, W_RET, W_RET, D_MODEL)
    offs = np.concatenate([[0], np.cumsum(sizes)])
    dq, dk, dv, sq, ckv, iq, ik, iw, rq, rk, rv, gate = (w[:, offs[i]:offs[i + 1]] for i in range(len(sizes)))
    pad = jnp.zeros((w.shape[0], LANES - D_IDX), w.dtype)
    main = jnp.concatenate([dq, dk, dv, sq, ckv, iq, ik, pad, rq, rk, rv, gate], axis=1).astype(jnp.bfloat16)
    wiw = jnp.concatenate([iw.T, jnp.zeros((16 - N_IDX_HEADS, w.shape[0]), w.dtype)], axis=0).astype(jnp.bfloat16)
    return main, wiw


def _block_diag_uk(w_uk):
    eye = jnp.eye(DSA_HEADS, dtype=w_uk.dtype)
    return jnp.einsum('hdr,hg->hdgr', w_uk, eye).reshape(W_DSA, W_QLAT).astype(jnp.bfloat16)


def kernel(x, attn_norm, w_in, diff_lambda, diff_norm, kv_norm, w_uk, w_uv, ret_norm, w_out, final_norm):
    batch, seq, d = x.shape
    assert d == D_MODEL and seq % TM == 0 and w_in.shape[0] == DEPTH
    top_k = min(DSA_TOPK_MAX, seq // 4)
    rc, rs = _rotary_tables(seq)
    ret_tables = _retention_tables()
    h = x.reshape(batch * seq, d)
    mix_args = None
    for layer in range(DEPTH):
        lam_init = 0.8 - 0.6 * math.exp(-0.3 * layer)
        w_main, wiw = _pack_w_in(w_in[layer])
        proj_args = (attn_norm[layer][None, :], w_main, wiw, _block_diag_uk(w_uk[layer]),
                     kv_norm[layer][None, :], rc, rs)
        outs = _layer_call(h, mix_args, proj_args, seq)
        if mix_args is not None:
            h, outs = outs[0], outs[1:]
        (dq, dk, dvt, qlat, ckv, ckvt, iq, ika, ikb, iwt, rq, rk, rv, g) = outs
        oa = _diff_call(diff_lambda[layer], dq, dk, dvt, diff_norm[layer][:, None], batch, seq, lam_init)
        ob = _dsa_call(iq, iwt, qlat, ika, ikb, ckv, ckvt,
                       jnp.swapaxes(w_uv[layer], 1, 2).astype(jnp.bfloat16), batch, seq, top_k)
        oc = _ret_call(rq, rk, rv, ret_tables, jnp.tile(ret_norm[layer], RET_HEADS)[None, :], batch, seq)
        mix_args = (g, oa.reshape(batch * seq, W_DIFF), ob.reshape(batch * seq, W_DSA), oc,
                    w_out[layer].astype(jnp.bfloat16), final_norm[None, :])
    (h,) = _layer_call(h, mix_args, None, seq, final=True)
    return h.reshape(batch, seq, d)
```

```python
import functools
import math

import numpy as np
import jax
import jax.numpy as jnp
from jax import lax
from jax.experimental import pallas as pl
from jax.experimental.pallas import tpu as pltpu

D_MODEL = 1024
DEPTH = 4
DIFF_HEADS = 6
DIFF_DK = 32
DIFF_DV = 64
DSA_HEADS = 6
DSA_DH = 64
DSA_LATENT = 128
N_IDX_HEADS = 8
D_IDX = 64
DSA_TOPK_MAX = 256
RET_HEADS = 4
RET_DK = 64
RET_DV = 64
RET_CHUNK = 256
ROPE_BASE = 10000.0
NORM_EPS = 1e-6
NEG = -1e30
IDX_SCALE = (N_IDX_HEADS ** -0.5) * (D_IDX ** -0.5)
LOG2E = 1.4426950408889634

W_DIFF = DIFF_HEADS * 2 * DIFF_DK
W_DSA = DSA_HEADS * DSA_DH
W_IDX = N_IDX_HEADS * D_IDX
W_RET = RET_HEADS * RET_DK
W_QLAT = DSA_HEADS * DSA_LATENT

C_DQ = 0
C_DK = C_DQ + W_DIFF
C_DV = C_DK + W_DIFF
C_SQ = C_DV + W_DIFF
C_CKV = C_SQ + W_DSA
C_IQ = C_CKV + DSA_LATENT
C_IK = C_IQ + W_IDX
C_RQ = C_IK + 128
C_RK = C_RQ + W_RET
C_RV = C_RK + W_RET
C_GATE = C_RV + W_RET
C_END = C_GATE + D_MODEL

LANES = 128
TILE = 256
TM = 512
VMEM_LIMIT = 56 * 1024 * 1024
MASKED = -3.0e38
M_INIT = -1.0e30
L_MIN = 2.0 ** -80
BAND = 2.0 ** -4
REFINE_STEPS = 16
EXTRACT_ROUNDS = 2
F32_TINY = 1.1754944e-38
INT_MIN = -2 ** 31

_NT = (((1,), (1,)), ((), ()))


def _neg_key():
    bits = int(np.array(NEG, np.float32).view(np.int32))
    return bits ^ ((bits >> 31) & 0x7FFFFFFF)


NEG_KEY = _neg_key()


N_MIX_IN = 6
N_PROJ_IN = 7
N_PROJ_OUT = 14


def _layer_kernel(*refs, has_mix, has_proj, final):
    refs = list(refs)
    h_ref = refs.pop(0)
    mix_in = [refs.pop(0) for _ in range(N_MIX_IN)] if has_mix else None
    proj_in = [refs.pop(0) for _ in range(N_PROJ_IN)] if has_proj else None
    h = h_ref[...]
    if has_mix:
        h = _mix_body(h, *mix_in, final=final)
        refs.pop(0)[...] = h
    if has_proj:
        _proj_body(h, *proj_in, *refs)


def _mix_body(h, g_ref, oa_ref, ob_ref, oc_ref, w_ref, fg_ref, *, final):
    mixed = jnp.concatenate([oa_ref[...], ob_ref[...], oc_ref[...]], axis=1)
    y = (g_ref[...].astype(jnp.float32) * mixed.astype(jnp.float32)).astype(jnp.bfloat16)
    h = h + jnp.dot(y, w_ref[...], preferred_element_type=jnp.float32)
    if final:
        ms = jnp.mean(h * h, axis=-1, keepdims=True)
        h = h * lax.rsqrt(ms + NORM_EPS) * fg_ref[...]
    return h


def _proj_body(h, gain_ref, w_ref, wiw_ref, wuk_ref, kvg_ref, rc_ref, rs_ref,
               dq_ref, dk_ref, dvt_ref, qlat_ref, ckv_ref, ckvt_ref, iq_ref, ika_ref, ikb_ref,
               iwt_ref, rq_ref, rk_ref, rv_ref, g_ref):
    f32, bf16 = jnp.float32, jnp.bfloat16
    ms = jnp.mean(h * h, axis=-1, keepdims=True)
    u = (h * lax.rsqrt(ms + NORM_EPS) * gain_ref[...]).astype(bf16)
    n_sub = h.shape[0] // TILE

    def proj(lo, hi):
        return jnp.dot(u, w_ref[:, lo:hi], preferred_element_type=f32)

    dq_ref[...] = (proj(C_DQ, C_DK) * (DIFF_DK ** -0.5 * LOG2E)).astype(bf16)
    dk_ref[...] = proj(C_DK, C_DV).astype(bf16)
    dv = proj(C_DV, C_SQ)
    for t in range(n_sub):
        dvt_ref[t] = dv[t * TILE:(t + 1) * TILE].T.astype(bf16)

    sq = proj(C_SQ, C_CKV).astype(bf16)
    qlat = jnp.dot(sq, wuk_ref[...], preferred_element_type=f32)
    qlat_ref[...] = (qlat * (DSA_DH ** -0.5 * LOG2E)).astype(bf16)
    ckv = proj(C_CKV, C_IQ)
    cms = jnp.mean(ckv * ckv, axis=-1, keepdims=True)
    c = ckv * lax.rsqrt(cms + NORM_EPS) * kvg_ref[...]
    ckv_ref[...] = c.astype(bf16)
    for t in range(n_sub):
        ckvt_ref[t] = c[t * TILE:(t + 1) * TILE].T.astype(bf16)

    iq_ref[...] = proj(C_IQ, C_IK).astype(bf16)
    ik = proj(C_IK, C_RQ)
    ika_ref[...] = ik.astype(bf16)
    ikb_ref[...] = pltpu.roll(ik, D_IDX, 1).astype(bf16)
    iwt = lax.dot_general(wiw_ref[...], u, _NT, preferred_element_type=f32) * IDX_SCALE
    for t in range(n_sub):
        iwt_ref[t] = iwt[:, t * TILE:(t + 1) * TILE]

    lane = lax.broadcasted_iota(jnp.int32, (h.shape[0], LANES), 1)
    first_half = (lane % RET_DK) < (RET_DK // 2)

    def rotary(x):
        parts = []
        for a in range(W_RET // LANES):
            xa = x[:, a * LANES:(a + 1) * LANES]
            partner = jnp.where(first_half, pltpu.roll(xa, LANES - RET_DK // 2, 1),
                                pltpu.roll(xa, RET_DK // 2, 1))
            sl = slice(a * LANES, (a + 1) * LANES)
            parts.append(xa * rc_ref[:, sl] + partner * rs_ref[:, sl])
        return jnp.concatenate(parts, axis=1)

    rq_ref[...] = rotary(proj(C_RQ, C_RK)).astype(bf16)
    rk_ref[...] = (rotary(proj(C_RK, C_RV)) * (RET_DK ** -0.5)).astype(bf16)
    rv_ref[...] = proj(C_RV, C_GATE).astype(bf16)

    gate = proj(C_GATE, C_END)
    g_ref[...] = (gate / (1.0 + jnp.exp(-gate))).astype(bf16)


def _layer_call(h, mix_args, proj_args, seq, final=False):
    n = h.shape[0]
    nt = n // TILE
    sub = TM // TILE
    pos_blocks = seq // TM
    bf16, f32 = jnp.bfloat16, jnp.float32
    row = lambda width: pl.BlockSpec((TM, width), lambda i: (i, 0))
    full = lambda a: pl.BlockSpec(a.shape, lambda i: (0,) * a.ndim, pipeline_mode=pl.Buffered(1))
    tposed = lambda rows: pl.BlockSpec((sub, rows, TILE), lambda i: (i, 0, 0))
    tab = pl.BlockSpec((TM, W_RET), lambda i: (i % pos_blocks, 0))
    args, in_specs, out_shape, out_specs = [h], [row(D_MODEL)], [], []
    if mix_args is not None:
        g, oa, ob, oc, w_out, fg = mix_args
        args += [g, oa, ob, oc, w_out, fg]
        in_specs += [row(D_MODEL), row(W_DIFF), row(W_DSA), row(W_RET), full(w_out), full(fg)]
        out_shape.append(jax.ShapeDtypeStruct((n, D_MODEL), f32))
        out_specs.append(row(D_MODEL))
    if proj_args is not None:
        gain, w, wiw, wuk, kvg, rc, rs = proj_args
        args += [gain, w, wiw, wuk, kvg, rc, rs]
        in_specs += [full(gain), full(w), full(wiw), full(wuk), full(kvg), tab, tab]
        out_shape += _proj_out_shapes(n, nt)
        out_specs += [
            row(W_DIFF), row(W_DIFF), tposed(W_DIFF), row(W_QLAT), row(DSA_LATENT), tposed(DSA_LATENT),
            row(W_IDX), row(LANES), row(LANES), tposed(16), row(W_RET), row(W_RET), row(W_RET), row(D_MODEL),
        ]
    return pl.pallas_call(
        functools.partial(_layer_kernel, has_mix=mix_args is not None, has_proj=proj_args is not None,
                          final=final),
        grid=(n // TM,),
        in_specs=in_specs,
        out_specs=tuple(out_specs),
        out_shape=tuple(out_shape),
        compiler_params=pltpu.CompilerParams(dimension_semantics=("parallel",),
                                             vmem_limit_bytes=VMEM_LIMIT),
        name="layer",
    )(*args)


def _proj_out_shapes(n, nt):
    bf16, f32 = jnp.bfloat16, jnp.float32
    return [
        jax.ShapeDtypeStruct((n, W_DIFF), bf16),
        jax.ShapeDtypeStruct((n, W_DIFF), bf16),
        jax.ShapeDtypeStruct((nt, W_DIFF, TILE), bf16),
        jax.ShapeDtypeStruct((n, W_QLAT), bf16),
        jax.ShapeDtypeStruct((n, DSA_LATENT), bf16),
        jax.ShapeDtypeStruct((nt, DSA_LATENT, TILE), bf16),
        jax.ShapeDtypeStruct((n, W_IDX), bf16),
        jax.ShapeDtypeStruct((n, LANES), bf16),
        jax.ShapeDtypeStruct((n, LANES), bf16),
        jax.ShapeDtypeStruct((nt, 16, TILE), f32),
        jax.ShapeDtypeStruct((n, W_RET), bf16),
        jax.ShapeDtypeStruct((n, W_RET), bf16),
        jax.ShapeDtypeStruct((n, W_RET), bf16),
        jax.ShapeDtypeStruct((n, D_MODEL), bf16),
    ]


def _col_groups(x):
    return x.reshape(x.shape[0] // 8, 8, x.shape[1])


def _tree_sum(xs):
    while len(xs) > 1:
        xs = [xs[i] + xs[i + 1] for i in range(0, len(xs) - 1, 2)] + ([xs[-1]] if len(xs) % 2 else [])
    return xs[0]


def _bit_transpose32(words):
    w = list(words)
    for j, mask in ((16, 0x0000FFFF), (8, 0x00FF00FF), (4, 0x0F0F0F0F), (2, 0x33333333), (1, 0x55555555)):
        for k in range(32):
            if k & j == 0:
                a, b = w[k], w[k + j]
                swap = ((a >> j) ^ b) & mask
                w[k + j] = b ^ swap
                w[k] = a ^ (swap << j)
    return w


def _group_rows(width, group):
    row = lax.broadcasted_iota(jnp.int32, (16, width), 0)
    lane = lax.broadcasted_iota(jnp.int32, (16, width), 1)
    return jnp.where((lane >= row * group) & (lane < (row + 1) * group), 1.0, 0.0).astype(jnp.bfloat16)


def _sq_norms_t(x, group_rows):
    xf = x.astype(jnp.float32)
    return lax.dot_general(group_rows, (xf * xf).astype(jnp.bfloat16), _NT, preferred_element_type=jnp.float32)


def _softmax_pv(n_streams, n_full, scores_fn, values_fn, bound, s_ref, acc_ref, per_tile=None):
    f32 = jnp.float32
    acc_ref[...] = jnp.zeros(acc_ref.shape, f32)

    def one_pass(j, diag, l8):
        if per_tile is not None:
            per_tile(j, diag)
        out = []
        vs = values_fn(j)
        for c, s in enumerate(scores_fn(j, diag)):
            p = jnp.exp2(s - bound[c:c + 1, :])
            out.append(l8[c] + jnp.sum(_col_groups(p), axis=0))
            acc_ref[c] += jnp.dot(vs[c], p.astype(jnp.bfloat16), preferred_element_type=f32)
        return tuple(out)

    l8 = tuple(jnp.zeros((8, TILE), f32) for _ in range(n_streams))
    l8 = lax.fori_loop(0, n_full // 2, lambda i, l: one_pass(2 * i + 1, False, one_pass(2 * i, False, l)), l8)
    l8 = lax.cond(n_full % 2 == 1, lambda l: one_pass(n_full - 1, False, l), lambda l: l, l8)
    l8 = one_pass(n_full, True, l8)
    fast_l = [jnp.sum(x, axis=0, keepdims=True) for x in l8]
    smallest = fast_l[0]
    for x in fast_l[1:]:
        smallest = jnp.minimum(smallest, x)
    trusted = jnp.min(jnp.where(smallest >= L_MIN, 1, 0)) > 0
    return lax.cond(trusted, lambda: fast_l,
                    lambda: _softmax_pv_exact(n_streams, n_full, scores_fn, values_fn, s_ref, acc_ref))


def _softmax_pv_exact(n_streams, n_full, scores_fn, values_fn, s_ref, acc_ref):
    f32 = jnp.float32
    acc_ref[...] = jnp.zeros(acc_ref.shape, f32)

    def pass_a(j, diag, m8):
        out = []
        for c, s in enumerate(scores_fn(j, diag)):
            s_ref[c, j] = s
            out.append(jnp.maximum(m8[c], jnp.max(_col_groups(s), axis=0)))
        return tuple(out)

    m8 = tuple(jnp.full((8, TILE), M_INIT, f32) for _ in range(n_streams))
    m8 = lax.fori_loop(0, n_full, lambda j, m: pass_a(j, False, m), m8)
    m8 = pass_a(n_full, True, m8)
    m = [jnp.max(x, axis=0, keepdims=True) for x in m8]

    def pass_b(j, l8):
        out = []
        for c, v in enumerate(values_fn(j)):
            p = jnp.exp2(s_ref[c, j] - m[c])
            out.append(l8[c] + jnp.sum(_col_groups(p), axis=0))
            acc_ref[c] += jnp.dot(v, p.astype(jnp.bfloat16), preferred_element_type=f32)
        return tuple(out)

    l8 = tuple(jnp.zeros((8, TILE), f32) for _ in range(n_streams))
    l8 = lax.fori_loop(0, n_full + 1, pass_b, l8)
    return [jnp.sum(x, axis=0, keepdims=True) for x in l8]


def _diff_body(qi, lam_ref, q_ref, k_ref, vt_ref, ng_ref, o_ref, s_ref, acc_ref, kmax_ref, *, lam_init,
               per_tile):
    f32 = jnp.float32
    n_maps = 2 * DIFF_HEADS
    maps_per_group = LANES // DIFF_DK
    q = q_ref[0]
    lane = lax.broadcasted_iota(jnp.int32, (TILE, LANES), 1)
    qm = []
    for c in range(n_maps):
        grp, sub = divmod(c, maps_per_group)
        qg = q[:, grp * LANES:(grp + 1) * LANES]
        qm.append(jnp.where((lane >= DIFF_DK * sub) & (lane < DIFF_DK * (sub + 1)), qg, jnp.zeros_like(qg)))
    visible = (lax.broadcasted_iota(jnp.int32, (TILE, TILE), 0)
               <= lax.broadcasted_iota(jnp.int32, (TILE, TILE), 1))

    def scores(j, diag):
        k_j = k_ref[0, j]
        out = []
        for c in range(n_maps):
            grp = c // maps_per_group
            s = lax.dot_general(k_j[:, grp * LANES:(grp + 1) * LANES], qm[c], _NT,
                                preferred_element_type=f32)
            out.append(jnp.where(visible, s, MASKED) if diag else s)
        return out

    def values(j):
        vt_j = vt_ref[0, j]
        return [vt_j[(c // 2) * DIFF_DV:(c // 2 + 1) * DIFF_DV, :] for c in range(n_maps)]

    groups = _group_rows(W_DIFF, DIFF_DK)

    @pl.when(qi == 0)
    def _():
        def tile_max(j, mx):
            return jnp.maximum(mx, jnp.max(_sq_norms_t(k_ref[0, j], groups), axis=1, keepdims=True))
        mx = lax.fori_loop(0, k_ref.shape[1], tile_max, jnp.zeros((16, 1), f32))
        kmax_ref[...] = jnp.broadcast_to(mx, kmax_ref.shape)

    bound = jnp.sqrt(_sq_norms_t(q, groups) * kmax_ref[:, 0:1])

    l = _softmax_pv(n_maps, qi, scores, values, bound, s_ref, acc_ref, per_tile=per_tile)

    lv = lam_ref[...]
    lam = (jnp.exp(jnp.sum(lv[0:1] * lv[1:2], axis=-1, keepdims=True))
           - jnp.exp(jnp.sum(lv[2:3] * lv[3:4], axis=-1, keepdims=True)) + lam_init)
    outs = []
    for hh in range(DIFF_HEADS):
        o = acc_ref[2 * hh] / l[2 * hh] - lam * (acc_ref[2 * hh + 1] / l[2 * hh + 1])
        ms = jnp.mean(o * o, axis=0, keepdims=True)
        outs.append(o * lax.rsqrt(ms + NORM_EPS) * ng_ref[...] * (1.0 - lam_init))
    o_ref[0] = jnp.concatenate(outs, axis=0).T.astype(o_ref.dtype)


def _dsa_fill_fn(qi, iq_ref, iwt_ref, ika_ref, ikb_ref, sc_ref, plane_ref, smax_ref):
    f32, i32 = jnp.float32, jnp.int32
    visible = (lax.broadcasted_iota(i32, (TILE, TILE), 0)
               <= lax.broadcasted_iota(i32, (TILE, TILE), 1))
    iq = iq_ref[0]
    iw = iwt_ref[0, 0]
    smax_ref[...] = jnp.zeros(smax_ref.shape, f32)

    @pl.when((pl.program_id(0) == 0) & (qi == 0))
    def _():
        plane_ref[...] = jnp.zeros(plane_ref.shape, i32)

    def fill(j, diag):
        ik_a = ika_ref[0, j]
        ik_b = ikb_ref[0, j]
        score = jnp.zeros((TILE, TILE), f32)
        for hd in range(N_IDX_HEADS):
            iq_pair = iq[:, (hd // 2) * LANES:(hd // 2 + 1) * LANES]
            x = lax.dot_general(ik_a if hd % 2 == 0 else ik_b, iq_pair, _NT, preferred_element_type=f32)
            score = score + jnp.maximum(x, 0.0) * iw[hd:hd + 1, :]
        mag = jnp.abs(score)
        if diag:
            mag = jnp.where(visible, mag, 0.0)
            score = jnp.where(visible, score, NEG)
        smax_ref[...] = jnp.maximum(smax_ref[...], jnp.max(_col_groups(mag), axis=0))
        score = jnp.where(jnp.abs(score) < F32_TINY, 0.0, score)
        sc_ref[j] = score
        bits = pltpu.bitcast(score, i32)
        key = bits ^ ((bits >> 31) & 0x7FFFFFFF)
        ukey = key ^ INT_MIN
        for p, plane in enumerate(_bit_transpose32([ukey[8 * i:8 * i + 8, :] for i in range(32)])):
            plane_ref[j, p] = plane

    return fill


def _dsa_select_attend(qi, qlat_ref, ckv_ref, ckvt_ref, wuvt_ref, o_ref, sc_ref, plane_ref, rb_ref, smax_ref,
                       s_ref, acc_ref, cmax_ref, *, top_k, seq):
    f32, bf16, i32 = jnp.float32, jnp.bfloat16, jnp.int32
    n_tiles = qi + 1
    rows = lax.broadcasted_iota(i32, (TILE, TILE), 0)
    visible = rows <= lax.broadcasted_iota(i32, (TILE, TILE), 1)

    n_masked = seq - n_tiles * TILE
    n_kv = sc_ref.shape[0]

    def bit_step(b, carry):
        alive, n_gt, neg_alive, t = carry
        p = 31 - b
        planes = [plane_ref[j, p] for j in range(n_kv)]
        ones8 = _tree_sum([lax.population_count(alive[j] & planes[j]) for j in range(n_kv)])
        neg_bit = jnp.right_shift(jnp.int32(NEG_KEY ^ INT_MIN), p) & 1
        n_one = jnp.sum(ones8, axis=0, keepdims=True) + neg_alive * (neg_bit * n_masked)
        take = (n_gt + n_one) >= top_k
        flip = jnp.where(take, 0, -1)
        alive = tuple(alive[j] & (planes[j] ^ flip) for j in range(n_kv))
        neg_alive = neg_alive & jnp.where(take, neg_bit, 1 - neg_bit)
        return (alive, jnp.where(take, n_gt, n_gt + n_one), neg_alive,
                t | jnp.where(take, jnp.left_shift(jnp.int32(1), p), 0))

    alive0 = tuple(jnp.broadcast_to(jnp.where(j <= qi, -1, 0).astype(i32), (8, TILE)) for j in range(n_kv))
    _, _, _, t = lax.fori_loop(
        0, 32, bit_step, (alive0, jnp.zeros((1, TILE), i32), jnp.ones((1, TILE), i32), jnp.zeros((1, TILE), i32)))
    t = t ^ INT_MIN
    t0 = pltpu.bitcast(t ^ ((t >> 31) & 0x7FFFFFFF), f32)
    smax = jnp.max(smax_ref[...], axis=0, keepdims=True)
    inv_band = 1.0 / jnp.maximum(BAND * smax, F32_TINY)

    def pack_residuals(j, carry):
        rb_ref[j] = jnp.clip((sc_ref[j] - t0) * inv_band, -2.0, 2.0).astype(bf16)
        return carry

    lax.fori_loop(0, n_tiles, pack_residuals, 0)
    r_masked = jnp.clip((NEG - t0) * inv_band, -2.0, 2.0)
    one, zero = jnp.ones((), bf16), jnp.zeros((), bf16)

    def count_ge(cand):
        cand16 = cand.astype(bf16)

        def tile_body(j, cnt):
            ind = jnp.where(rb_ref[j] >= cand16, one, zero)
            return cnt + _tree_sum([ind[r:r + 16] for r in range(0, TILE, 16)])
        cnt16 = lax.fori_loop(0, n_tiles, tile_body, jnp.zeros((16, TILE), bf16))
        cnt = jnp.sum(cnt16.astype(f32), axis=0, keepdims=True).astype(i32)
        return cnt + jnp.where(r_masked >= cand, n_masked, 0)

    def next_above(tau):
        tau16 = tau.astype(bf16)
        none = jnp.full((), 4.0, bf16)

        def tile_body(j, low):
            x = rb_ref[j]
            x = jnp.where(x > tau16, x, none)
            parts = [x[r:r + 16] for r in range(0, TILE, 16)]
            while len(parts) > 1:
                parts = [jnp.minimum(parts[i], parts[i + 1]) for i in range(0, len(parts), 2)]
            return jnp.minimum(low, parts[0])
        low16 = lax.fori_loop(0, n_tiles, tile_body, jnp.full((16, TILE), 4.0, bf16))
        low = jnp.min(low16.astype(f32), axis=0, keepdims=True)
        return jnp.minimum(low, jnp.where(r_masked > tau, r_masked, 4.0))

    def refine(b, tau):
        cand = tau + pltpu.bitcast(jnp.full((1, TILE), 127, i32) - b << 23, f32)
        return jnp.where(count_ge(cand) >= top_k, cand, tau)

    tau = lax.fori_loop(0, REFINE_STEPS, refine, jnp.full((1, TILE), -1.0, f32))
    for _ in range(EXTRACT_ROUNDS):
        nxt = next_above(tau)
        tau = jnp.where(count_ge(nxt) >= top_k, nxt, tau)

    def mark_ties(j, cnt):
        tied = jnp.where((sc_ref[j] - t0) * inv_band == tau, 1.0, 0.0)
        s_ref[0, j] = tied
        return cnt + jnp.sum(_col_groups(tied), axis=0)

    n_eq = jnp.sum(lax.fori_loop(0, n_tiles, mark_ties, jnp.zeros((8, TILE), f32)),
                   axis=0, keepdims=True).astype(i32)
    n_ge = count_ge(tau)
    room = top_k - (n_ge - n_eq - jnp.where(r_masked == tau, n_masked, 0))
    pos_bits = seq.bit_length()

    def tie_break():
        def pos_body(b, lim):
            cand = lim + jnp.left_shift(jnp.int32(1), pos_bits - b)

            def tile_body(j, cnt):
                below = jnp.where(rows + j * TILE < cand, s_ref[0, j], 0.0)
                return cnt + jnp.sum(_col_groups(below), axis=0)
            cnt = jnp.sum(lax.fori_loop(0, n_tiles, tile_body, jnp.zeros((8, TILE), f32)),
                          axis=0, keepdims=True).astype(i32)
            return jnp.where(cnt <= room, cand, lim)
        return lax.fori_loop(0, pos_bits + 1, pos_body, jnp.zeros((1, TILE), i32))

    need = jnp.max(jnp.where(n_eq > room, 1, 0)) > 0
    pos_lim = lax.cond(need, tie_break, lambda: jnp.full((1, TILE), 2 * seq, i32))

    qlat = qlat_ref[0]

    def scores(j, diag):
        r = (sc_ref[j] - t0) * inv_band
        tied = jnp.where(rows + j * TILE < pos_lim, 0.0, MASKED)
        bias = jnp.where(r > tau, 0.0, jnp.where(r == tau, tied, MASKED))
        if diag:
            bias = jnp.where(visible, bias, MASKED)
        c_j = ckv_ref[0, j]
        return [lax.dot_general(c_j, qlat[:, hd * DSA_LATENT:(hd + 1) * DSA_LATENT], _NT,
                                preferred_element_type=f32) + bias for hd in range(DSA_HEADS)]

    def values(j):
        return [ckvt_ref[0, j]] * DSA_HEADS

    @pl.when(qi == 0)
    def _():
        ones = jnp.ones((16, DSA_LATENT), bf16)

        def tile_max(j, mx):
            return jnp.maximum(mx, jnp.max(_sq_norms_t(ckv_ref[0, j], ones), axis=1, keepdims=True))
        mx = lax.fori_loop(0, n_kv, tile_max, jnp.zeros((16, 1), f32))
        cmax_ref[...] = jnp.broadcast_to(mx, cmax_ref.shape)

    bound = jnp.sqrt(_sq_norms_t(qlat, _group_rows(W_QLAT, DSA_LATENT)) * cmax_ref[:, 0:1])

    l = _softmax_pv(DSA_HEADS, qi, scores, values, bound, s_ref, acc_ref)

    outs = []
    for hd in range(DSA_HEADS):
        o_lat = (acc_ref[hd] / l[hd]).astype(bf16)
        outs.append(jnp.dot(wuvt_ref[hd], o_lat, preferred_element_type=f32))
    o_ref[0] = jnp.concatenate(outs, axis=0).T.astype(o_ref.dtype)


def _attn_kernel(lam_ref, dq_ref, dk_ref, dvt_ref, ng_ref, iq_ref, iwt_ref, qlat_ref, ika_ref, ikb_ref,
                 ckv_ref, ckvt_ref, wuvt_ref, oa_ref, ob_ref,
                 s_ref, acc_a_ref, kmax_ref, sc_ref, plane_ref, rb_ref, smax_ref, acc_b_ref, cmax_ref,
                 *, lam_init, top_k, seq):
    qi = pl.program_id(1)
    fill = _dsa_fill_fn(qi, iq_ref, iwt_ref, ika_ref, ikb_ref, sc_ref, plane_ref, smax_ref)
    _diff_body(qi, lam_ref, dq_ref, dk_ref, dvt_ref, ng_ref, oa_ref, s_ref, acc_a_ref, kmax_ref,
               lam_init=lam_init, per_tile=fill)
    _dsa_select_attend(qi, qlat_ref, ckv_ref, ckvt_ref, wuvt_ref, ob_ref, sc_ref, plane_ref, rb_ref, smax_ref,
                       s_ref, acc_b_ref, cmax_ref, top_k=top_k, seq=seq)


def _attn_call(lam_vecs, dq, dk, dvt, ng, iq, iwt, qlat, ika, ikb, ckv, ckvt, wuvt, batch, seq, lam_init, top_k):
    nt = seq // TILE
    kv4 = lambda a: a.reshape(batch, nt, TILE, a.shape[-1])
    const = lambda a: pl.BlockSpec(a.shape, lambda b, i: (0,) * a.ndim)
    q_tile = lambda width: pl.BlockSpec((1, TILE, width), lambda b, i: (b, i, 0))
    kv_rows = lambda width: pl.BlockSpec((1, nt, TILE, width), lambda b, i: (b, 0, 0, 0))
    kv_cols = lambda rows: pl.BlockSpec((1, nt, rows, TILE), lambda b, i: (b, 0, 0, 0))
    return pl.pallas_call(
        functools.partial(_attn_kernel, lam_init=lam_init, top_k=top_k, seq=seq),
        grid=(batch, nt),
        in_specs=[
            const(lam_vecs), q_tile(W_DIFF), kv_rows(W_DIFF), kv_cols(W_DIFF), const(ng),
            q_tile(W_IDX), pl.BlockSpec((1, 1, 16, TILE), lambda b, i: (b, i, 0, 0)), q_tile(W_QLAT),
            kv_rows(LANES), kv_rows(LANES), kv_rows(DSA_LATENT), kv_cols(DSA_LATENT), const(wuvt),
        ],
        out_specs=(q_tile(W_DIFF), q_tile(W_DSA)),
        out_shape=(jax.ShapeDtypeStruct((batch, seq, W_DIFF), jnp.bfloat16),
                   jax.ShapeDtypeStruct((batch, seq, W_DSA), jnp.bfloat16)),
        scratch_shapes=[
            pltpu.VMEM((2 * DIFF_HEADS, nt, TILE, TILE), jnp.float32),
            pltpu.VMEM((2 * DIFF_HEADS, DIFF_DV, TILE), jnp.float32),
            pltpu.VMEM((16, LANES), jnp.float32),
            pltpu.VMEM((nt, TILE, TILE), jnp.float32),
            pltpu.VMEM((nt, 32, 8, TILE), jnp.int32),
            pltpu.VMEM((nt, TILE, TILE), jnp.bfloat16),
            pltpu.VMEM((8, TILE), jnp.float32),
            pltpu.VMEM((DSA_HEADS, DSA_LATENT, TILE), jnp.float32),
            pltpu.VMEM((16, LANES), jnp.float32),
        ],
        compiler_params=pltpu.CompilerParams(
            dimension_semantics=("arbitrary", "arbitrary"), vmem_limit_bytes=VMEM_LIMIT),
        name="attn",
    )(lam_vecs, dq.reshape(batch, seq, W_DIFF), kv4(dk), dvt.reshape(batch, nt, W_DIFF, TILE), ng,
      iq.reshape(batch, seq, W_IDX), iwt.reshape(batch, nt, 16, TILE), qlat.reshape(batch, seq, W_QLAT),
      kv4(ika), kv4(ikb), kv4(ckv), ckvt.reshape(batch, nt, DSA_LATENT, TILE), wuvt)


def _ret_kernel(q_ref, k_ref, v_ref, dintra_ref, xi_ref, zeta_ref, gmat_ref, bd_ref, ng_ref, o_ref,
                state_ref, *, n_chunks):
    f32, bf16 = jnp.float32, jnp.bfloat16
    lane = lax.broadcasted_iota(jnp.int32, (RET_CHUNK, W_RET), 1)
    head_of_lane = lane // RET_DK
    state_ref[...] = jnp.zeros(state_ref.shape, f32)
    ones_bd = bd_ref[...].astype(bf16)

    def chunk(ci, carry):
        for r in range(q_ref.shape[0]):
            chunk_row(r, ci)
        return carry

    def chunk_row(r, ci):
        q = q_ref[r, ci]
        k = k_ref[r, ci]
        v = v_ref[r, ci]
        state = state_ref[r]
        inner = jnp.zeros((RET_CHUNK, W_RET), f32)
        for hd in range(RET_HEADS):
            mine = head_of_lane == hd
            att = lax.dot_general(jnp.where(mine, q, jnp.zeros_like(q)), k, _NT,
                                  preferred_element_type=f32) * dintra_ref[hd]
            inner = inner + jnp.dot(att.astype(bf16), jnp.where(mine, v, jnp.zeros_like(v)),
                                    preferred_element_type=f32)
        cross = jnp.dot(q, state.astype(bf16), preferred_element_type=f32) * xi_ref[...]
        kz = (k.astype(f32) * zeta_ref[...]).T.astype(bf16)
        state_ref[r] = state * gmat_ref[...] + jnp.dot(kz, v, preferred_element_type=f32) * bd_ref[...]
        o = inner + cross
        ss = jnp.dot((o * o).astype(bf16), ones_bd, preferred_element_type=f32)
        o_ref[r, ci] = (o * lax.rsqrt(ss * (1.0 / RET_DV) + NORM_EPS) * ng_ref[...]).astype(o_ref.dtype)

    lax.fori_loop(0, n_chunks, chunk, 0)


def _ret_call(rq, rk, rv, tables, ng, batch, seq):
    nc = seq // RET_CHUNK
    rows = 2 if batch % 2 == 0 else 1
    dintra, xi, zeta, gmat, bd = tables
    r4 = lambda a: a.reshape(batch, nc, RET_CHUNK, W_RET)
    blk = pl.BlockSpec((rows, nc, RET_CHUNK, W_RET), lambda b: (b, 0, 0, 0))
    full = lambda a: pl.BlockSpec(a.shape, lambda b: (0,) * a.ndim)
    out = pl.pallas_call(
        functools.partial(_ret_kernel, n_chunks=nc),
        grid=(batch // rows,),
        in_specs=[blk, blk, blk, full(dintra), full(xi), full(zeta), full(gmat), full(bd), full(ng)],
        out_specs=blk,
        out_shape=jax.ShapeDtypeStruct((batch, nc, RET_CHUNK, W_RET), jnp.bfloat16),
        scratch_shapes=[pltpu.VMEM((rows, W_RET, W_RET), jnp.float32)],
        compiler_params=pltpu.CompilerParams(dimension_semantics=("parallel",),
                                             vmem_limit_bytes=VMEM_LIMIT),
        name="retention",
    )(r4(rq), r4(rk), r4(rv), dintra, xi, zeta, gmat, bd, ng)
    return out.reshape(batch * seq, W_RET)


def _rotary_tables(seq):
    inv_freq = ROPE_BASE ** (-jnp.arange(RET_DK // 2, dtype=jnp.float32) / (RET_DK // 2))
    ang = jnp.arange(seq, dtype=jnp.float32)[:, None] * inv_freq[None, :]
    cos, sin = jnp.cos(ang), jnp.sin(ang)
    rc = jnp.tile(jnp.concatenate([cos, cos], axis=1), (1, RET_HEADS))
    rs = jnp.tile(jnp.concatenate([-sin, sin], axis=1), (1, RET_HEADS))
    return rc, rs


def _retention_tables():
    c = RET_CHUNK
    log_g = jnp.log(1.0 - 2.0 ** (-5.0 - jnp.arange(RET_HEADS, dtype=jnp.float32)))
    pos = jnp.arange(c, dtype=jnp.float32)
    diff = pos[:, None] - pos[None, :]
    dintra = jnp.where(diff >= 0, jnp.exp(jnp.maximum(diff, 0.0)[None] * log_g[:, None, None]), 0.0)
    xi = jnp.repeat(jnp.exp((pos + 1.0)[:, None] * log_g[None, :]), RET_DK, axis=1)
    zeta = jnp.repeat(jnp.exp((c - 1.0 - pos)[:, None] * log_g[None, :]), RET_DK, axis=1)
    head = jnp.arange(W_RET) // RET_DK
    bd = (head[:, None] == head[None, :]).astype(jnp.float32)
    gmat = bd * jnp.exp(c * log_g)[head][:, None]
    return dintra, xi, zeta, gmat, bd


def _pack_w_in(w):
    sizes = (W_DIFF, W_DIFF, W_DIFF, W_DSA, DSA_LATENT, W_IDX, D_IDX, N_IDX_HEADS, W_RET, W_RET, W_RET, D_MODEL)
    offs = np.concatenate([[0], np.cumsum(sizes)])
    dq, dk, dv, sq, ckv, iq, ik, iw, rq, rk, rv, gate = (w[:, offs[i]:offs[i + 1]] for i in range(len(sizes)))
    pad = jnp.zeros((w.shape[0], LANES - D_IDX), w.dtype)
    main = jnp.concatenate([dq, dk, dv, sq, ckv, iq, ik, pad, rq, rk, rv, gate], axis=1).astype(jnp.bfloat16)
    wiw = jnp.concatenate([iw.T, jnp.zeros((16 - N_IDX_HEADS, w.shape[0]), w.dtype)], axis=0).astype(jnp.bfloat16)
    return main, wiw


def _block_diag_uk(w_uk):
    eye = jnp.eye(DSA_HEADS, dtype=w_uk.dtype)
    return jnp.einsum('hdr,hg->hdgr', w_uk, eye).reshape(W_DSA, W_QLAT).astype(jnp.bfloat16)


def kernel(x, attn_norm, w_in, diff_lambda, diff_norm, kv_norm, w_uk, w_uv, ret_norm, w_out, final_norm):
    batch, seq, d = x.shape
    assert d == D_MODEL and seq % TM == 0 and w_in.shape[0] == DEPTH
    top_k = min(DSA_TOPK_MAX, seq // 4)
    rc, rs = _rotary_tables(seq)
    ret_tables = _retention_tables()
    h = x.reshape(batch * seq, d)
    mix_args = None
    for layer in range(DEPTH):
        lam_init = 0.8 - 0.6 * math.exp(-0.3 * layer)
        w_main, wiw = _pack_w_in(w_in[layer])
        proj_args = (attn_norm[layer][None, :], w_main, wiw, _block_diag_uk(w_uk[layer]),
                     kv_norm[layer][None, :], rc, rs)
        outs = _layer_call(h, mix_args, proj_args, seq)
        if mix_args is not None:
            h, outs = outs[0], outs[1:]
        (dq, dk, dvt, qlat, ckv, ckvt, iq, ika, ikb, iwt, rq, rk, rv, g) = outs
        oa, ob = _attn_call(diff_lambda[layer], dq, dk, dvt, diff_norm[layer][:, None],
                            iq, iwt, qlat, ika, ikb, ckv, ckvt,
                            jnp.swapaxes(w_uv[layer], 1, 2).astype(jnp.bfloat16), batch, seq, lam_init, top_k)
        oc = _ret_call(rq, rk, rv, ret_tables, jnp.tile(ret_norm[layer], RET_HEADS)[None, :], batch, seq)
        mix_args = (g, oa.reshape(batch * seq, W_DIFF), ob.reshape(batch * seq, W_DSA), oc,
                    w_out[layer].astype(jnp.bfloat16), final_norm[None, :])
    (h,) = _layer_call(h, mix_args, None, seq, final=True)
    return h.reshape(batch, seq, d)
```

```python
import functools
import math

import numpy as np
import jax
import jax.numpy as jnp
from jax import lax
from jax.experimental import pallas as pl
from jax.experimental.pallas import tpu as pltpu

D_MODEL = 1024
DEPTH = 4
DIFF_HEADS = 6
DIFF_DK = 32
DIFF_DV = 64
DSA_HEADS = 6
DSA_DH = 64
DSA_LATENT = 128
N_IDX_HEADS = 8
D_IDX = 64
DSA_TOPK_MAX = 256
RET_HEADS = 4
RET_DK = 64
RET_DV = 64
RET_CHUNK = 256
ROPE_BASE = 10000.0
NORM_EPS = 1e-6
NEG = -1e30
IDX_SCALE = (N_IDX_HEADS ** -0.5) * (D_IDX ** -0.5)
LOG2E = 1.4426950408889634

W_DIFF = DIFF_HEADS * 2 * DIFF_DK
W_DSA = DSA_HEADS * DSA_DH
W_IDX = N_IDX_HEADS * D_IDX
W_RET = RET_HEADS * RET_DK
W_QLAT = DSA_HEADS * DSA_LATENT

C_DQ = 0
C_DK = C_DQ + W_DIFF
C_DV = C_DK + W_DIFF
C_SQ = C_DV + W_DIFF
C_CKV = C_SQ + W_DSA
C_IQ = C_CKV + DSA_LATENT
C_IK = C_IQ + W_IDX
C_RQ = C_IK + 128
C_RK = C_RQ + W_RET
C_RV = C_RK + W_RET
C_GATE = C_RV + W_RET
C_END = C_GATE + D_MODEL

LANES = 128
TILE = 256
TM = 512
VMEM_LIMIT = 56 * 1024 * 1024
MASKED = -3.0e38
M_INIT = -1.0e30
L_MIN = 2.0 ** -80
BAND = 2.0 ** -4
REFINE_STEPS = 16
EXTRACT_ROUNDS = 2
F32_TINY = 1.1754944e-38
INT_MIN = -2 ** 31

_NT = (((1,), (1,)), ((), ()))


def _neg_key():
    bits = int(np.array(NEG, np.float32).view(np.int32))
    return bits ^ ((bits >> 31) & 0x7FFFFFFF)


NEG_KEY = _neg_key()


N_MIX_IN = 6
N_PROJ_IN = 7
N_PROJ_OUT = 14


def _layer_kernel(*refs, has_mix, has_proj, final):
    refs = list(refs)
    h_ref = refs.pop(0)
    mix_in = [refs.pop(0) for _ in range(N_MIX_IN)] if has_mix else None
    proj_in = [refs.pop(0) for _ in range(N_PROJ_IN)] if has_proj else None
    h = h_ref[...]
    if has_mix:
        h = _mix_body(h, *mix_in, final=final)
        refs.pop(0)[...] = h
    if has_proj:
        _proj_body(h, *proj_in, *refs)


def _mix_body(h, g_ref, oa_ref, ob_ref, oc_ref, w_ref, fg_ref, *, final):
    mixed = jnp.concatenate([oa_ref[...], ob_ref[...], oc_ref[...]], axis=1)
    y = (g_ref[...].astype(jnp.float32) * mixed.astype(jnp.float32)).astype(jnp.bfloat16)
    h = h + jnp.dot(y, w_ref[...], preferred_element_type=jnp.float32)
    if final:
        ms = jnp.mean(h * h, axis=-1, keepdims=True)
        h = h * lax.rsqrt(ms + NORM_EPS) * fg_ref[...]
    return h


def _proj_body(h, gain_ref, w_ref, wiw_ref, wuk_ref, kvg_ref, rc_ref, rs_ref,
               dq_ref, dk_ref, dvt_ref, qlat_ref, ckv_ref, ckvt_ref, iq_ref, ika_ref, ikb_ref,
               iwt_ref, rq_ref, rk_ref, rv_ref, g_ref):
    f32, bf16 = jnp.float32, jnp.bfloat16
    ms = jnp.mean(h * h, axis=-1, keepdims=True)
    u = (h * lax.rsqrt(ms + NORM_EPS) * gain_ref[...]).astype(bf16)
    n_sub = h.shape[0] // TILE

    def proj(lo, hi):
        return jnp.dot(u, w_ref[:, lo:hi], preferred_element_type=f32)

    dq_ref[...] = (proj(C_DQ, C_DK) * (DIFF_DK ** -0.5 * LOG2E)).astype(bf16)
    dk_ref[...] = proj(C_DK, C_DV).astype(bf16)
    dv = proj(C_DV, C_SQ)
    for t in range(n_sub):
        dvt_ref[t] = dv[t * TILE:(t + 1) * TILE].T.astype(bf16)

    sq = proj(C_SQ, C_CKV).astype(bf16)
    qlat = jnp.dot(sq, wuk_ref[...], preferred_element_type=f32)
    qlat_ref[...] = (qlat * (DSA_DH ** -0.5 * LOG2E)).astype(bf16)
    ckv = proj(C_CKV, C_IQ)
    cms = jnp.mean(ckv * ckv, axis=-1, keepdims=True)
    c = ckv * lax.rsqrt(cms + NORM_EPS) * kvg_ref[...]
    ckv_ref[...] = c.astype(bf16)
    for t in range(n_sub):
        ckvt_ref[t] = c[t * TILE:(t + 1) * TILE].T.astype(bf16)

    iq_ref[...] = proj(C_IQ, C_IK).astype(bf16)
    ik = proj(C_IK, C_RQ)
    ika_ref[...] = ik.astype(bf16)
    ikb_ref[...] = pltpu.roll(ik, D_IDX, 1).astype(bf16)
    iwt = lax.dot_general(wiw_ref[...], u, _NT, preferred_element_type=f32) * IDX_SCALE
    for t in range(n_sub):
        iwt_ref[t] = iwt[:, t * TILE:(t + 1) * TILE]

    lane = lax.broadcasted_iota(jnp.int32, (h.shape[0], LANES), 1)
    first_half = (lane % RET_DK) < (RET_DK // 2)

    def rotary(x):
        parts = []
        for a in range(W_RET // LANES):
            xa = x[:, a * LANES:(a + 1) * LANES]
            partner = jnp.where(first_half, pltpu.roll(xa, LANES - RET_DK // 2, 1),
                                pltpu.roll(xa, RET_DK // 2, 1))
            sl = slice(a * LANES, (a + 1) * LANES)
            parts.append(xa * rc_ref[:, sl] + partner * rs_ref[:, sl])
        return jnp.concatenate(parts, axis=1)

    rq_ref[...] = rotary(proj(C_RQ, C_RK)).astype(bf16)
    rk_ref[...] = (rotary(proj(C_RK, C_RV)) * (RET_DK ** -0.5)).astype(bf16)
    rv_ref[...] = proj(C_RV, C_GATE).astype(bf16)

    gate = proj(C_GATE, C_END)
    g_ref[...] = (gate / (1.0 + jnp.exp(-gate))).astype(bf16)


def _layer_call(h, mix_args, proj_args, seq, final=False):
    n = h.shape[0]
    nt = n // TILE
    sub = TM // TILE
    pos_blocks = seq // TM
    bf16, f32 = jnp.bfloat16, jnp.float32
    row = lambda width: pl.BlockSpec((TM, width), lambda i: (i, 0))
    full = lambda a: pl.BlockSpec(a.shape, lambda i: (0,) * a.ndim, pipeline_mode=pl.Buffered(1))
    tposed = lambda rows: pl.BlockSpec((sub, rows, TILE), lambda i: (i, 0, 0))
    tab = pl.BlockSpec((TM, W_RET), lambda i: (i % pos_blocks, 0))
    args, in_specs, out_shape, out_specs = [h], [row(D_MODEL)], [], []
    if mix_args is not None:
        g, oa, ob, oc, w_out, fg = mix_args
        args += [g, oa, ob, oc, w_out, fg]
        in_specs += [row(D_MODEL), row(W_DIFF), row(W_DSA), row(W_RET), full(w_out), full(fg)]
        out_shape.append(jax.ShapeDtypeStruct((n, D_MODEL), f32))
        out_specs.append(row(D_MODEL))
    if proj_args is not None:
        gain, w, wiw, wuk, kvg, rc, rs = proj_args
        args += [gain, w, wiw, wuk, kvg, rc, rs]
        in_specs += [full(gain), full(w), full(wiw), full(wuk), full(kvg), tab, tab]
        out_shape += _proj_out_shapes(n, nt)
        out_specs += [
            row(W_DIFF), row(W_DIFF), tposed(W_DIFF), row(W_QLAT), row(DSA_LATENT), tposed(DSA_LATENT),
            row(W_IDX), row(LANES), row(LANES), tposed(16), row(W_RET), row(W_RET), row(W_RET), row(D_MODEL),
        ]
    return pl.pallas_call(
        functools.partial(_layer_kernel, has_mix=mix_args is not None, has_proj=proj_args is not None,
                          final=final),
        grid=(n // TM,),
        in_specs=in_specs,
        out_specs=tuple(out_specs),
        out_shape=tuple(out_shape),
        compiler_params=pltpu.CompilerParams(dimension_semantics=("parallel",),
                                             vmem_limit_bytes=VMEM_LIMIT),
        name="layer",
    )(*args)


def _proj_out_shapes(n, nt):
    bf16, f32 = jnp.bfloat16, jnp.float32
    return [
        jax.ShapeDtypeStruct((n, W_DIFF), bf16),
        jax.ShapeDtypeStruct((n, W_DIFF), bf16),
        jax.ShapeDtypeStruct((nt, W_DIFF, TILE), bf16),
        jax.ShapeDtypeStruct((n, W_QLAT), bf16),
        jax.ShapeDtypeStruct((n, DSA_LATENT), bf16),
        jax.ShapeDtypeStruct((nt, DSA_LATENT, TILE), bf16),
        jax.ShapeDtypeStruct((n, W_IDX), bf16),
        jax.ShapeDtypeStruct((n, LANES), bf16),
        jax.ShapeDtypeStruct((n, LANES), bf16),
        jax.ShapeDtypeStruct((nt, 16, TILE), f32),
        jax.ShapeDtypeStruct((n, W_RET), bf16),
        jax.ShapeDtypeStruct((n, W_RET), bf16),
        jax.ShapeDtypeStruct((n, W_RET), bf16),
        jax.ShapeDtypeStruct((n, D_MODEL), bf16),
    ]


def _col_groups(x):
    return x.reshape(x.shape[0] // 8, 8, x.shape[1])


def _tree_sum(xs):
    while len(xs) > 1:
        xs = [xs[i] + xs[i + 1] for i in range(0, len(xs) - 1, 2)] + ([xs[-1]] if len(xs) % 2 else [])
    return xs[0]


def _bit_transpose32(words):
    w = list(words)
    for j, mask in ((16, 0x0000FFFF), (8, 0x00FF00FF), (4, 0x0F0F0F0F), (2, 0x33333333), (1, 0x55555555)):
        for k in range(32):
            if k & j == 0:
                a, b = w[k], w[k + j]
                swap = ((a >> j) ^ b) & mask
                w[k + j] = b ^ swap
                w[k] = a ^ (swap << j)
    return w


def _group_rows(width, group):
    row = lax.broadcasted_iota(jnp.int32, (16, width), 0)
    lane = lax.broadcasted_iota(jnp.int32, (16, width), 1)
    return jnp.where((lane >= row * group) & (lane < (row + 1) * group), 1.0, 0.0).astype(jnp.bfloat16)


def _sq_norms_t(x, group_rows):
    xf = x.astype(jnp.float32)
    return lax.dot_general(group_rows, (xf * xf).astype(jnp.bfloat16), _NT, preferred_element_type=jnp.float32)


def _softmax_pv(n_streams, n_full, scores_fn, values_fn, bound, s_ref, acc_ref, per_tile=None):
    f32 = jnp.float32
    acc_ref[...] = jnp.zeros(acc_ref.shape, f32)

    def one_pass(j, diag, l8):
        if per_tile is not None:
            per_tile(j, diag)
        out = []
        vs = values_fn(j)
        for c, s in enumerate(scores_fn(j, diag)):
            p = jnp.exp2(s - bound[c:c + 1, :])
            out.append(l8[c] + jnp.sum(_col_groups(p), axis=0))
            acc_ref[c] += jnp.dot(vs[c], p.astype(jnp.bfloat16), preferred_element_type=f32)
        return tuple(out)

    l8 = tuple(jnp.zeros((8, TILE), f32) for _ in range(n_streams))
    l8 = lax.fori_loop(0, n_full // 2, lambda i, l: one_pass(2 * i + 1, False, one_pass(2 * i, False, l)), l8)
    l8 = lax.cond(n_full % 2 == 1, lambda l: one_pass(n_full - 1, False, l), lambda l: l, l8)
    l8 = one_pass(n_full, True, l8)
    fast_l = [jnp.sum(x, axis=0, keepdims=True) for x in l8]
    smallest = fast_l[0]
    for x in fast_l[1:]:
        smallest = jnp.minimum(smallest, x)
    trusted = jnp.min(jnp.where(smallest >= L_MIN, 1, 0)) > 0
    return lax.cond(trusted, lambda: fast_l,
                    lambda: _softmax_pv_exact(n_streams, n_full, scores_fn, values_fn, s_ref, acc_ref))


def _softmax_pv_exact(n_streams, n_full, scores_fn, values_fn, s_ref, acc_ref):
    f32 = jnp.float32
    acc_ref[...] = jnp.zeros(acc_ref.shape, f32)

    def pass_a(j, diag, m8):
        out = []
        for c, s in enumerate(scores_fn(j, diag)):
            s_ref[c, j] = s
            out.append(jnp.maximum(m8[c], jnp.max(_col_groups(s), axis=0)))
        return tuple(out)

    m8 = tuple(jnp.full((8, TILE), M_INIT, f32) for _ in range(n_streams))
    m8 = lax.fori_loop(0, n_full, lambda j, m: pass_a(j, False, m), m8)
    m8 = pass_a(n_full, True, m8)
    m = [jnp.max(x, axis=0, keepdims=True) for x in m8]

    def pass_b(j, l8):
        out = []
        for c, v in enumerate(values_fn(j)):
            p = jnp.exp2(s_ref[c, j] - m[c])
            out.append(l8[c] + jnp.sum(_col_groups(p), axis=0))
            acc_ref[c] += jnp.dot(v, p.astype(jnp.bfloat16), preferred_element_type=f32)
        return tuple(out)

    l8 = tuple(jnp.zeros((8, TILE), f32) for _ in range(n_streams))
    l8 = lax.fori_loop(0, n_full + 1, pass_b, l8)
    return [jnp.sum(x, axis=0, keepdims=True) for x in l8]


def _diff_body(qi, lam_ref, q_ref, k_ref, vt_ref, ng_ref, o_ref, s_ref, acc_ref, kmax_ref, *, lam_init,
               per_tile):
    f32 = jnp.float32
    n_maps = 2 * DIFF_HEADS
    maps_per_group = LANES // DIFF_DK
    q = q_ref[0]
    lane = lax.broadcasted_iota(jnp.int32, (TILE, LANES), 1)
    qm = []
    for c in range(n_maps):
        grp, sub = divmod(c, maps_per_group)
        qg = q[:, grp * LANES:(grp + 1) * LANES]
        qm.append(jnp.where((lane >= DIFF_DK * sub) & (lane < DIFF_DK * (sub + 1)), qg, jnp.zeros_like(qg)))
    visible = (lax.broadcasted_iota(jnp.int32, (TILE, TILE), 0)
               <= lax.broadcasted_iota(jnp.int32, (TILE, TILE), 1))

    def scores(j, diag):
        k_j = k_ref[0, j]
        out = []
        for c in range(n_maps):
            grp = c // maps_per_group
            s = lax.dot_general(k_j[:, grp * LANES:(grp + 1) * LANES], qm[c], _NT,
                                preferred_element_type=f32)
            out.append(jnp.where(visible, s, MASKED) if diag else s)
        return out

    def values(j):
        vt_j = vt_ref[0, j]
        return [vt_j[(c // 2) * DIFF_DV:(c // 2 + 1) * DIFF_DV, :] for c in range(n_maps)]

    groups = _group_rows(W_DIFF, DIFF_DK)

    @pl.when(qi == 0)
    def _():
        def tile_max(j, mx):
            return jnp.maximum(mx, jnp.max(_sq_norms_t(k_ref[0, j], groups), axis=1, keepdims=True))
        mx = lax.fori_loop(0, k_ref.shape[1], tile_max, jnp.zeros((16, 1), f32))
        kmax_ref[...] = jnp.broadcast_to(mx, kmax_ref.shape)

    bound = jnp.sqrt(_sq_norms_t(q, groups) * kmax_ref[:, 0:1])

    l = _softmax_pv(n_maps, qi, scores, values, bound, s_ref, acc_ref, per_tile=per_tile)

    lv = lam_ref[...]
    lam = (jnp.exp(jnp.sum(lv[0:1] * lv[1:2], axis=-1, keepdims=True))
           - jnp.exp(jnp.sum(lv[2:3] * lv[3:4], axis=-1, keepdims=True)) + lam_init)
    outs = []
    for hh in range(DIFF_HEADS):
        o = acc_ref[2 * hh] / l[2 * hh] - lam * (acc_ref[2 * hh + 1] / l[2 * hh + 1])
        ms = jnp.mean(o * o, axis=0, keepdims=True)
        outs.append(o * lax.rsqrt(ms + NORM_EPS) * ng_ref[...] * (1.0 - lam_init))
    o_ref[0] = jnp.concatenate(outs, axis=0).T.astype(o_ref.dtype)


def _dsa_fill_fn(qi, iq_ref, iwt_ref, ika_ref, ikb_ref, sc_ref, plane_ref, smax_ref):
    f32, i32 = jnp.float32, jnp.int32
    visible = (lax.broadcasted_iota(i32, (TILE, TILE), 0)
               <= lax.broadcasted_iota(i32, (TILE, TILE), 1))
    iq = iq_ref[0]
    iw = iwt_ref[0, 0]
    smax_ref[...] = jnp.zeros(smax_ref.shape, f32)

    @pl.when((pl.program_id(0) == 0) & (qi == 0))
    def _():
        plane_ref[...] = jnp.zeros(plane_ref.shape, i32)

    def fill(j, diag):
        ik_a = ika_ref[0, j]
        ik_b = ikb_ref[0, j]
        score = jnp.zeros((TILE, TILE), f32)
        for hd in range(N_IDX_HEADS):
            iq_pair = iq[:, (hd // 2) * LANES:(hd // 2 + 1) * LANES]
            x = lax.dot_general(ik_a if hd % 2 == 0 else ik_b, iq_pair, _NT, preferred_element_type=f32)
            score = score + jnp.maximum(x, 0.0) * iw[hd:hd + 1, :]
        mag = jnp.abs(score)
        if diag:
            mag = jnp.where(visible, mag, 0.0)
            score = jnp.where(visible, score, NEG)
        smax_ref[...] = jnp.maximum(smax_ref[...], jnp.max(_col_groups(mag), axis=0))
        score = jnp.where(jnp.abs(score) < F32_TINY, 0.0, score)
        sc_ref[j] = score
        bits = pltpu.bitcast(score, i32)
        key = bits ^ ((bits >> 31) & 0x7FFFFFFF)
        ukey = key ^ INT_MIN
        for p, plane in enumerate(_bit_transpose32([ukey[8 * i:8 * i + 8, :] for i in range(32)])):
            plane_ref[j, p] = plane

    return fill


def _dsa_select_attend(qi, qlat_ref, ckv_ref, ckvt_ref, wuvt_ref, o_ref, sc_ref, plane_ref, rb_ref, smax_ref,
                       tie_ref, s_ref, acc_ref, cmax_ref, *, top_k, seq):
    f32, bf16, i32 = jnp.float32, jnp.bfloat16, jnp.int32
    n_tiles = qi + 1
    rows = lax.broadcasted_iota(i32, (TILE, TILE), 0)
    visible = rows <= lax.broadcasted_iota(i32, (TILE, TILE), 1)

    n_masked = seq - n_tiles * TILE
    n_kv = sc_ref.shape[0]

    def bit_step(b, carry):
        alive, n_gt, neg_alive, t = carry
        p = 31 - b
        planes = [plane_ref[j, p] for j in range(n_kv)]
        ones8 = _tree_sum([lax.population_count(alive[j] & planes[j]) for j in range(n_kv)])
        neg_bit = jnp.right_shift(jnp.int32(NEG_KEY ^ INT_MIN), p) & 1
        n_one = jnp.sum(ones8, axis=0, keepdims=True) + neg_alive * (neg_bit * n_masked)
        take = (n_gt + n_one) >= top_k
        flip = jnp.where(take, 0, -1)
        alive = tuple(alive[j] & (planes[j] ^ flip) for j in range(n_kv))
        neg_alive = neg_alive & jnp.where(take, neg_bit, 1 - neg_bit)
        return (alive, jnp.where(take, n_gt, n_gt + n_one), neg_alive,
                t | jnp.where(take, jnp.left_shift(jnp.int32(1), p), 0))

    alive0 = tuple(jnp.broadcast_to(jnp.where(j <= qi, -1, 0).astype(i32), (8, TILE)) for j in range(n_kv))
    _, _, _, t = lax.fori_loop(
        0, 32, bit_step, (alive0, jnp.zeros((1, TILE), i32), jnp.ones((1, TILE), i32), jnp.zeros((1, TILE), i32)))
    t = t ^ INT_MIN
    t0 = pltpu.bitcast(t ^ ((t >> 31) & 0x7FFFFFFF), f32)
    smax = jnp.max(smax_ref[...], axis=0, keepdims=True)
    inv_band = 1.0 / jnp.maximum(BAND * smax, F32_TINY)

    def pack_residuals(j, carry):
        rb_ref[j] = jnp.clip((sc_ref[j] - t0) * inv_band, -2.0, 2.0).astype(bf16)
        return carry

    lax.fori_loop(0, n_tiles, pack_residuals, 0)
    r_masked = jnp.clip((NEG - t0) * inv_band, -2.0, 2.0)
    one, zero = jnp.ones((), bf16), jnp.zeros((), bf16)

    def count_ge(cand):
        cand16 = cand.astype(bf16)

        def tile_body(j, cnt):
            ind = jnp.where(rb_ref[j] >= cand16, one, zero)
            return cnt + _tree_sum([ind[r:r + 16] for r in range(0, TILE, 16)])
        cnt16 = lax.fori_loop(0, n_tiles, tile_body, jnp.zeros((16, TILE), bf16))
        cnt = jnp.sum(cnt16.astype(f32), axis=0, keepdims=True).astype(i32)
        return cnt + jnp.where(r_masked >= cand, n_masked, 0)

    def next_above(tau):
        tau16 = tau.astype(bf16)
        none = jnp.full((), 4.0, bf16)

        def tile_body(j, low):
            x = rb_ref[j]
            x = jnp.where(x > tau16, x, none)
            parts = [x[r:r + 16] for r in range(0, TILE, 16)]
            while len(parts) > 1:
                parts = [jnp.minimum(parts[i], parts[i + 1]) for i in range(0, len(parts), 2)]
            return jnp.minimum(low, parts[0])
        low16 = lax.fori_loop(0, n_tiles, tile_body, jnp.full((16, TILE), 4.0, bf16))
        low = jnp.min(low16.astype(f32), axis=0, keepdims=True)
        return jnp.minimum(low, jnp.where(r_masked > tau, r_masked, 4.0))

    def refine(b, tau):
        cand = tau + pltpu.bitcast(jnp.full((1, TILE), 127, i32) - b << 23, f32)
        return jnp.where(count_ge(cand) >= top_k, cand, tau)

    tau = lax.fori_loop(0, REFINE_STEPS, refine, jnp.full((1, TILE), -1.0, f32))
    for _ in range(EXTRACT_ROUNDS):
        nxt = next_above(tau)
        tau = jnp.where(count_ge(nxt) >= top_k, nxt, tau)

    def tied_at(j):
        return (sc_ref[j] - t0) * inv_band == tau

    def count_ties(j, cnt):
        return cnt + jnp.sum(_col_groups(jnp.where(tied_at(j), 1, 0)), axis=0)

    n_eq = jnp.sum(lax.fori_loop(0, n_tiles, count_ties, jnp.zeros((8, TILE), i32)), axis=0, keepdims=True)
    n_ge = count_ge(tau)
    room = top_k - (n_ge - n_eq - jnp.where(r_masked == tau, n_masked, 0))
    pos_bits = seq.bit_length()

    def tie_break():
        tie_ref[...] = jnp.zeros(tie_ref.shape, i32)

        def pack_ties(j, carry):
            tied = jnp.where(tied_at(j), 1, 0)
            word = tied[0:8]
            for i in range(1, 32):
                word = word | jnp.left_shift(tied[8 * i:8 * i + 8], i)
            tie_ref[j] = word
            return carry

        lax.fori_loop(0, n_tiles, pack_ties, 0)
        words = [tie_ref[j] for j in range(n_kv)]
        sublane = lax.broadcasted_iota(i32, (8, TILE), 0)

        def pos_body(b, lim):
            cand = lim + jnp.left_shift(jnp.int32(1), pos_bits - b)
            below8 = []
            for j in range(n_kv):
                n_below = jnp.clip((cand - (TILE * j - 7) - sublane) >> 3, 0, 32)
                mask = jnp.where(n_below >= 32, -1, jnp.left_shift(jnp.int32(1), jnp.minimum(n_below, 31)) - 1)
                below8.append(lax.population_count(words[j] & mask))
            cnt = jnp.sum(_tree_sum(below8), axis=0, keepdims=True)
            return jnp.where(cnt <= room, cand, lim)
        return lax.fori_loop(0, pos_bits + 1, pos_body, jnp.zeros((1, TILE), i32))

    need = jnp.max(jnp.where(n_eq > room, 1, 0)) > 0
    pos_lim = lax.cond(need, tie_break, lambda: jnp.full((1, TILE), 2 * seq, i32))

    qlat = qlat_ref[0]

    def scores(j, diag):
        r = (sc_ref[j] - t0) * inv_band
        tied = jnp.where(rows + j * TILE < pos_lim, 0.0, MASKED)
        bias = jnp.where(r > tau, 0.0, jnp.where(r == tau, tied, MASKED))
        if diag:
            bias = jnp.where(visible, bias, MASKED)
        c_j = ckv_ref[0, j]
        return [lax.dot_general(c_j, qlat[:, hd * DSA_LATENT:(hd + 1) * DSA_LATENT], _NT,
                                preferred_element_type=f32) + bias for hd in range(DSA_HEADS)]

    def values(j):
        return [ckvt_ref[0, j]] * DSA_HEADS

    @pl.when(qi == 0)
    def _():
        ones = jnp.ones((16, DSA_LATENT), bf16)

        def tile_max(j, mx):
            return jnp.maximum(mx, jnp.max(_sq_norms_t(ckv_ref[0, j], ones), axis=1, keepdims=True))
        mx = lax.fori_loop(0, n_kv, tile_max, jnp.zeros((16, 1), f32))
        cmax_ref[...] = jnp.broadcast_to(mx, cmax_ref.shape)

    bound = jnp.sqrt(_sq_norms_t(qlat, _group_rows(W_QLAT, DSA_LATENT)) * cmax_ref[:, 0:1])

    l = _softmax_pv(DSA_HEADS, qi, scores, values, bound, s_ref, acc_ref)

    outs = []
    for hd in range(DSA_HEADS):
        o_lat = (acc_ref[hd] / l[hd]).astype(bf16)
        outs.append(jnp.dot(wuvt_ref[hd], o_lat, preferred_element_type=f32))
    o_ref[0] = jnp.concatenate(outs, axis=0).T.astype(o_ref.dtype)


def _attn_kernel(lam_ref, dq_ref, dk_ref, dvt_ref, ng_ref, iq_ref, iwt_ref, qlat_ref, ika_ref, ikb_ref,
                 ckv_ref, ckvt_ref, wuvt_ref, oa_ref, ob_ref,
                 s_ref, acc_a_ref, kmax_ref, sc_ref, plane_ref, rb_ref, smax_ref, tie_ref, acc_b_ref, cmax_ref,
                 *, lam_init, top_k, seq):
    qi = pl.program_id(1)
    fill = _dsa_fill_fn(qi, iq_ref, iwt_ref, ika_ref, ikb_ref, sc_ref, plane_ref, smax_ref)
    _diff_body(qi, lam_ref, dq_ref, dk_ref, dvt_ref, ng_ref, oa_ref, s_ref, acc_a_ref, kmax_ref,
               lam_init=lam_init, per_tile=fill)
    _dsa_select_attend(qi, qlat_ref, ckv_ref, ckvt_ref, wuvt_ref, ob_ref, sc_ref, plane_ref, rb_ref, smax_ref,
                       tie_ref, s_ref, acc_b_ref, cmax_ref, top_k=top_k, seq=seq)


def _attn_call(lam_vecs, dq, dk, dvt, ng, iq, iwt, qlat, ika, ikb, ckv, ckvt, wuvt, batch, seq, lam_init, top_k):
    nt = seq // TILE
    kv4 = lambda a: a.reshape(batch, nt, TILE, a.shape[-1])
    const = lambda a: pl.BlockSpec(a.shape, lambda b, i: (0,) * a.ndim)
    q_tile = lambda width: pl.BlockSpec((1, TILE, width), lambda b, i: (b, i, 0))
    kv_rows = lambda width: pl.BlockSpec((1, nt, TILE, width), lambda b, i: (b, 0, 0, 0))
    kv_cols = lambda rows: pl.BlockSpec((1, nt, rows, TILE), lambda b, i: (b, 0, 0, 0))
    return pl.pallas_call(
        functools.partial(_attn_kernel, lam_init=lam_init, top_k=top_k, seq=seq),
        grid=(batch, nt),
        in_specs=[
            const(lam_vecs), q_tile(W_DIFF), kv_rows(W_DIFF), kv_cols(W_DIFF), const(ng),
            q_tile(W_IDX), pl.BlockSpec((1, 1, 16, TILE), lambda b, i: (b, i, 0, 0)), q_tile(W_QLAT),
            kv_rows(LANES), kv_rows(LANES), kv_rows(DSA_LATENT), kv_cols(DSA_LATENT), const(wuvt),
        ],
        out_specs=(q_tile(W_DIFF), q_tile(W_DSA)),
        out_shape=(jax.ShapeDtypeStruct((batch, seq, W_DIFF), jnp.bfloat16),
                   jax.ShapeDtypeStruct((batch, seq, W_DSA), jnp.bfloat16)),
        scratch_shapes=[
            pltpu.VMEM((2 * DIFF_HEADS, nt, TILE, TILE), jnp.float32),
            pltpu.VMEM((2 * DIFF_HEADS, DIFF_DV, TILE), jnp.float32),
            pltpu.VMEM((16, LANES), jnp.float32),
            pltpu.VMEM((nt, TILE, TILE), jnp.float32),
            pltpu.VMEM((nt, 32, 8, TILE), jnp.int32),
            pltpu.VMEM((nt, TILE, TILE), jnp.bfloat16),
            pltpu.VMEM((8, TILE), jnp.float32),
            pltpu.VMEM((nt, 8, TILE), jnp.int32),
            pltpu.VMEM((DSA_HEADS, DSA_LATENT, TILE), jnp.float32),
            pltpu.VMEM((16, LANES), jnp.float32),
        ],
        compiler_params=pltpu.CompilerParams(
            dimension_semantics=("arbitrary", "arbitrary"), vmem_limit_bytes=VMEM_LIMIT),
        name="attn",
    )(lam_vecs, dq.reshape(batch, seq, W_DIFF), kv4(dk), dvt.reshape(batch, nt, W_DIFF, TILE), ng,
      iq.reshape(batch, seq, W_IDX), iwt.reshape(batch, nt, 16, TILE), qlat.reshape(batch, seq, W_QLAT),
      kv4(ika), kv4(ikb), kv4(ckv), ckvt.reshape(batch, nt, DSA_LATENT, TILE), wuvt)


def _ret_kernel(q_ref, k_ref, v_ref, dintra_ref, xi_ref, zeta_ref, gmat_ref, bd_ref, ng_ref, o_ref,
                state_ref, *, n_chunks):
    f32, bf16 = jnp.float32, jnp.bfloat16
    lane = lax.broadcasted_iota(jnp.int32, (RET_CHUNK, W_RET), 1)
    head_of_lane = lane // RET_DK
    state_ref[...] = jnp.zeros(state_ref.shape, f32)
    ones_bd = bd_ref[...].astype(bf16)

    def chunk(ci, carry):
        for r in range(q_ref.shape[0]):
            chunk_row(r, ci)
        return carry

    def chunk_row(r, ci):
        q = q_ref[r, ci]
        k = k_ref[r, ci]
        v = v_ref[r, ci]
        state = state_ref[r]
        inner = jnp.zeros((RET_CHUNK, W_RET), f32)
        for hd in range(RET_HEADS):
            mine = head_of_lane == hd
            att = lax.dot_general(jnp.where(mine, q, jnp.zeros_like(q)), k, _NT,
                                  preferred_element_type=f32) * dintra_ref[hd]
            inner = inner + jnp.dot(att.astype(bf16), jnp.where(mine, v, jnp.zeros_like(v)),
                                    preferred_element_type=f32)
        cross = jnp.dot(q, state.astype(bf16), preferred_element_type=f32) * xi_ref[...]
        kz = (k.astype(f32) * zeta_ref[...]).T.astype(bf16)
        state_ref[r] = state * gmat_ref[...] + jnp.dot(kz, v, preferred_element_type=f32) * bd_ref[...]
        o = inner + cross
        ss = jnp.dot((o * o).astype(bf16), ones_bd, preferred_element_type=f32)
        o_ref[r, ci] = (o * lax.rsqrt(ss * (1.0 / RET_DV) + NORM_EPS) * ng_ref[...]).astype(o_ref.dtype)

    lax.fori_loop(0, n_chunks, chunk, 0)


def _ret_call(rq, rk, rv, tables, ng, batch, seq):
    nc = seq // RET_CHUNK
    rows = 2 if batch % 2 == 0 else 1
    dintra, xi, zeta, gmat, bd = tables
    r4 = lambda a: a.reshape(batch, nc, RET_CHUNK, W_RET)
    blk = pl.BlockSpec((rows, nc, RET_CHUNK, W_RET), lambda b: (b, 0, 0, 0))
    full = lambda a: pl.BlockSpec(a.shape, lambda b: (0,) * a.ndim)
    out = pl.pallas_call(
        functools.partial(_ret_kernel, n_chunks=nc),
        grid=(batch // rows,),
        in_specs=[blk, blk, blk, full(dintra), full(xi), full(zeta), full(gmat), full(bd), full(ng)],
        out_specs=blk,
        out_shape=jax.ShapeDtypeStruct((batch, nc, RET_CHUNK, W_RET), jnp.bfloat16),
        scratch_shapes=[pltpu.VMEM((rows, W_RET, W_RET), jnp.float32)],
        compiler_params=pltpu.CompilerParams(dimension_semantics=("parallel",),
                                             vmem_limit_bytes=VMEM_LIMIT),
        name="retention",
    )(r4(rq), r4(rk), r4(rv), dintra, xi, zeta, gmat, bd, ng)
    return out.reshape(batch * seq, W_RET)


def _rotary_tables(seq):
    inv_freq = ROPE_BASE ** (-jnp.arange(RET_DK // 2, dtype=jnp.float32) / (RET_DK // 2))
    ang = jnp.arange(seq, dtype=jnp.float32)[:, None] * inv_freq[None, :]
    cos, sin = jnp.cos(ang), jnp.sin(ang)
    rc = jnp.tile(jnp.concatenate([cos, cos], axis=1), (1, RET_HEADS))
    rs = jnp.tile(jnp.concatenate([-sin, sin], axis=1), (1, RET_HEADS))
    return rc, rs


def _retention_tables():
    c = RET_CHUNK
    log_g = jnp.log(1.0 - 2.0 ** (-5.0 - jnp.arange(RET_HEADS, dtype=jnp.float32)))
    pos = jnp.arange(c, dtype=jnp.float32)
    diff = pos[:, None] - pos[None, :]
    dintra = jnp.where(diff >= 0, jnp.exp(jnp.maximum(diff, 0.0)[None] * log_g[:, None, None]), 0.0)
    xi = jnp.repeat(jnp.exp((pos + 1.0)[:, None] * log_g[None, :]), RET_DK, axis=1)
    zeta = jnp.repeat(jnp.exp((c - 1.0 - pos)[:, None] * log_g[None, :]), RET_DK, axis=1)
    head = jnp.arange(W_RET) // RET_DK
    bd = (head[:, None] == head[None, :]).astype(jnp.float32)
    gmat = bd * jnp.exp(c * log_g)[head][:, None]
    return dintra, xi, zeta, gmat, bd


def _pack_w_in(w):
    sizes = (W_DIFF, W_DIFF, W_DIFF, W_DSA, DSA_LATENT, W_IDX, D_IDX, N_IDX_HEADS, W_RET, W_RET, W_RET, D_MODEL)
    offs = np.concatenate([[0], np.cumsum(sizes)])
    dq, dk, dv, sq, ckv, iq, ik, iw, rq, rk, rv, gate = (w[:, offs[i]:offs[i + 1]] for i in range(len(sizes)))
    pad = jnp.zeros((w.shape[0], LANES - D_IDX), w.dtype)
    main = jnp.concatenate([dq, dk, dv, sq, ckv, iq, ik, pad, rq, rk, rv, gate], axis=1).astype(jnp.bfloat16)
    wiw = jnp.concatenate([iw.T, jnp.zeros((16 - N_IDX_HEADS, w.shape[0]), w.dtype)], axis=0).astype(jnp.bfloat16)
    return main, wiw


def _block_diag_uk(w_uk):
    eye = jnp.eye(DSA_HEADS, dtype=w_uk.dtype)
    return jnp.einsum('hdr,hg->hdgr', w_uk, eye).reshape(W_DSA, W_QLAT).astype(jnp.bfloat16)


def kernel(x, attn_norm, w_in, diff_lambda, diff_norm, kv_norm, w_uk, w_uv, ret_norm, w_out, final_norm):
    batch, seq, d = x.shape
    assert d == D_MODEL and seq % TM == 0 and w_in.shape[0] == DEPTH
    top_k = min(DSA_TOPK_MAX, seq // 4)
    rc, rs = _rotary_tables(seq)
    ret_tables = _retention_tables()
    h = x.reshape(batch * seq, d)
    mix_args = None
    for layer in range(DEPTH):
        lam_init = 0.8 - 0.6 * math.exp(-0.3 * layer)
        w_main, wiw = _pack_w_in(w_in[layer])
        proj_args = (attn_norm[layer][None, :], w_main, wiw, _block_diag_uk(w_uk[layer]),
                     kv_norm[layer][None, :], rc, rs)
        outs = _layer_call(h, mix_args, proj_args, seq)
        if mix_args is not None:
            h, outs = outs[0], outs[1:]
        (dq, dk, dvt, qlat, ckv, ckvt, iq, ika, ikb, iwt, rq, rk, rv, g) = outs
        oa, ob = _attn_call(diff_lambda[layer], dq, dk, dvt, diff_norm[layer][:, None],
                            iq, iwt, qlat, ika, ikb, ckv, ckvt,
                            jnp.swapaxes(w_uv[layer], 1, 2).astype(jnp.bfloat16), batch, seq, lam_init, top_k)
        oc = _ret_call(rq, rk, rv, ret_tables, jnp.tile(ret_norm[layer], RET_HEADS)[None, :], batch, seq)
        mix_args = (g, oa.reshape(batch * seq, W_DIFF), ob.reshape(batch * seq, W_DSA), oc,
                    w_out[layer].astype(jnp.bfloat16), final_norm[None, :])
    (h,) = _layer_call(h, mix_args, None, seq, final=True)
    return h.reshape(batch, seq, d)
```
